```python
import math
import jax
import jax.numpy as jnp
from jax import lax
import numpy as np

D_MODEL = 2048
BATCH = 8
SEQ = 2048
DEPTH = 2
DEC_BATCH = 32
DEC_SEQ = 4
PAST_LEN = 8192
PAGE_SIZE = 128

BRANCH_W = D_MODEL // 2
A_VD = 128
A_HD = A_VD // 2
A_HEADS = BRANCH_W // A_VD
A_KVH = A_HEADS // 2
A_GRP = A_HEADS // A_KVH
B_HD = 128
B_HEADS = BRANCH_W // B_HD
B_KVH = B_HEADS // 2
B_GRP = B_HEADS // B_KVH
C_HD = 128
C_HEADS = BRANCH_W // C_HD
C_KVH = C_HEADS // 4
C_GRP = C_HEADS // C_KVH
L_BLK = 64
N_SEL = 16
WINDOW = 512
REL_BUCKETS = 32
REL_MAX_DIST = 128
Q_BLOCK = 128
EPS = 1e-6
NEG = -1e30
FAR = 1 << 30
FORCE = 1e6

SPLITS = (A_HEADS * 2 * A_HD, A_KVH * 2 * A_HD, A_KVH * A_VD, BRANCH_W,
          B_HEADS * B_HD, B_KVH * B_HD, B_KVH * B_HD, B_HEADS, BRANCH_W,
          C_HEADS * C_HD, 6 * C_KVH * C_HD, 3 * C_HEADS, BRANCH_W,
          3 * D_MODEL)
N_IN = sum(SPLITS)

kernel_name = 'hybrid_diff_fox_nsa_decode_step'


def rms_norm(x, g):
    xf = x.astype(jnp.float32)
    y = xf * lax.rsqrt(jnp.mean(xf * xf, axis=-1, keepdims=True) + EPS)
    return (y * g.astype(jnp.float32)).astype(x.dtype)


def split_cols(z):
    idx = np.cumsum(SPLITS)[:-1].tolist()
    return jnp.split(z, idx, axis=-1)


def t5_bucket(dist):
    n = jnp.maximum(dist, 0)
    exact = REL_BUCKETS // 2
    nf = jnp.maximum(n, exact).astype(jnp.float32)
    big = exact + (jnp.log(nf / exact) / math.log(REL_MAX_DIST / exact)
                   * (REL_BUCKETS - exact)).astype(jnp.int32)
    return jnp.where(n < exact, n, jnp.minimum(big, REL_BUCKETS - 1))


def sweep_blocks(fn, q_pos, q_arrays):
    T = q_pos.shape[0]
    blk = Q_BLOCK if T % Q_BLOCK == 0 else T
    nb = T // blk
    if nb == 1:
        return fn(q_pos, *q_arrays)
    xs = (q_pos.reshape(nb, blk),) + tuple(a.reshape((nb, blk) + a.shape[1:]) for a in q_arrays)
    out = lax.map(lambda a: fn(*a), xs)
    return out.reshape((T,) + out.shape[2:])


def compress(rows, pe, w1, w2):
    n_cmp = rows.shape[0] // L_BLK
    blocks = rows[: n_cmp * L_BLK].reshape(n_cmp, L_BLK, C_KVH, C_HD) + pe[None, :, None, :]
    flat = blocks.transpose(0, 2, 1, 3).reshape(n_cmp, C_KVH, L_BLK * C_HD)
    return jax.nn.silu(flat @ w1) @ w2


def diff_attn_block(qp, q, k, v, lam, rel_a):
    nq, S = q.shape[0], k.shape[0]
    dist = qp[:, None] - jnp.arange(S)[None, :]
    qg = q.reshape(nq, A_KVH, A_GRP, 2, A_HD)
    s = jnp.einsum('qkgcd,skcd->kgcqs', qg, k, preferred_element_type=jnp.float32) * (A_HD ** -0.5)
    bias = rel_a[t5_bucket(dist)].astype(jnp.float32).reshape(nq, S, A_KVH, A_GRP, 2)
    s = jnp.where(dist >= 0, s + jnp.transpose(bias, (2, 3, 4, 0, 1)), NEG)
    p = jax.nn.softmax(s, axis=-1)
    pd = p[:, :, 0] - lam * p[:, :, 1]
    o = jnp.einsum('kgqs,skd->qkgd', pd, v.astype(jnp.float32))
    return o.reshape(nq, A_HEADS, A_VD)


def fox_block(qp, q, cq, k, v, ck):
    nq, S = q.shape[0], k.shape[0]
    qg = q.reshape(nq, B_KVH, B_GRP, B_HD)
    s = jnp.einsum('qkgd,skd->kgqs', qg, k, preferred_element_type=jnp.float32) * (B_HD ** -0.5)
    decay = (cq.reshape(nq, B_KVH, B_GRP).transpose(1, 2, 0)[..., None]
             - ck.reshape(S, B_KVH, B_GRP).transpose(1, 2, 0)[:, :, None, :])
    causal = jnp.arange(S)[None, :] <= qp[:, None]
    p = jax.nn.softmax(jnp.where(causal, s + decay, NEG), axis=-1)
    o = jnp.einsum('kgqs,skd->qkgd', p, v.astype(jnp.float32))
    return o.reshape(nq, B_HEADS, B_HD)


def nsa_block(qp, q, gates, kc, vc, cmp_end, ks, vs, wkv, wpos, w_first, rel_c):
    nq = q.shape[0]
    f32 = jnp.float32
    scale = C_HD ** -0.5
    qg = q.reshape(nq, C_KVH, C_GRP, C_HD)
    rel3 = rel_c.astype(f32).reshape(REL_BUCKETS, C_KVH, C_GRP)
    dc = qp[:, None] - cmp_end[None, :]
    sc = (jnp.einsum('qkgd,nkd->kgqn', qg, kc, preferred_element_type=f32) * scale
          + jnp.transpose(rel3[t5_bucket(dc)], (2, 3, 0, 1)))
    okc = dc >= 0
    pc = jnp.where(okc, jax.nn.softmax(jnp.where(okc, sc, NEG), axis=-1), 0.0)
    o_cmp = jnp.einsum('kgqn,nkd->qkgd', pc, vc.astype(f32))
    n_cmp, n_sel = kc.shape[0], ks.shape[1]
    imp = jnp.pad(pc.sum(axis=1), ((0, 0), (0, 0), (0, n_sel - n_cmp)))
    j = jnp.arange(n_sel)[None, :]
    cur = (qp // L_BLK)[:, None]
    imp = imp + FORCE * (jnp.where(j == cur, 4.0, 0.0) + jnp.where(j == cur - 1, 2.0, 0.0)
                         + jnp.where(j == 0, 1.0, 0.0))
    imp = jnp.where(j <= cur, imp, NEG)
    _, idx = lax.top_k(imp, min(N_SEL, n_sel))
    head = jnp.arange(C_KVH)[:, None, None]
    kg = ks[head, idx]
    vg = vs[head, idx]
    spos = idx[..., None] * L_BLK + jnp.arange(L_BLK)
    ds = qp[None, :, None, None] - spos
    nk = idx.shape[-1] * L_BLK
    ss = (jnp.einsum('qkgd,kqjld->kgqjl', qg, kg, preferred_element_type=f32) * scale
          + jnp.moveaxis(rel3[t5_bucket(ds), head[..., None]], -1, 1))
    ss = jnp.where((ds >= 0)[:, None], ss, NEG).reshape(C_KVH, C_GRP, nq, nk)
    ps = jax.nn.softmax(ss, axis=-1)
    o_sel = jnp.einsum('kgqm,kqmd->qkgd', ps, vg.reshape(C_KVH, nq, nk, C_HD).astype(f32))
    start = qp[0] - w_first
    band = lax.dynamic_slice_in_dim(wkv, start, WINDOW + nq, axis=0)
    bpos = lax.dynamic_slice_in_dim(wpos, start, WINDOW + nq, axis=0)
    dw = qp[:, None] - bpos[None, :]
    sw = (jnp.einsum('qkgd,skd->kgqs', qg, band[:, :, 0], preferred_element_type=f32) * scale
          + jnp.transpose(rel3[t5_bucket(dw)], (2, 3, 0, 1)))
    sw = jnp.where((dw >= 0) & (dw <= WINDOW), sw, NEG)
    pw = jax.nn.softmax(sw, axis=-1)
    o_win = jnp.einsum('kgqs,skd->qkgd', pw, band[:, :, 1].astype(f32))
    g = gates.astype(f32).reshape(nq, C_KVH, C_GRP, 3)
    o = g[..., 0:1] * o_cmp + g[..., 1:2] * o_sel + g[..., 2:3] * o_win
    return o.reshape(nq, C_HEADS, C_HD)


def mixer_layer(x, pos0, win_buf, win_keep, past, lam_init,
                rms_g, w_in, b_f, a_qn, a_kn, a_lam, a_subln, b_qn, b_kn,
                c_qn, c_kn, c_pe, c_w1, c_w2, rel_bias, w_branch, w_out):
    B, T, _ = x.shape
    dt = x.dtype
    f32 = jnp.float32
    h = rms_norm(x, rms_g)
    (a_q, a_k, a_v, a_g, b_q, b_k, b_v, b_fl, b_g,
     c_q, c_kvp, c_gl, c_g, m_g) = split_cols(h @ w_in)
    aq = rms_norm(a_q.reshape(B, T, A_HEADS, 2, A_HD), a_qn)
    ak = rms_norm(a_k.reshape(B, T, A_KVH, 2, A_HD), a_kn)
    a_new = jnp.concatenate([ak.reshape(B, T, A_KVH, 2 * A_HD), a_v.reshape(B, T, A_KVH, A_VD)], axis=-1)
    lam = (jnp.exp(jnp.sum(a_lam[0].astype(f32) * a_lam[1].astype(f32)))
           - jnp.exp(jnp.sum(a_lam[2].astype(f32) * a_lam[3].astype(f32))) + lam_init)
    bq = rms_norm(b_q.reshape(B, T, B_HEADS, B_HD), b_qn)
    bk = rms_norm(b_k.reshape(B, T, B_KVH, B_HD), b_kn)
    b_new = jnp.concatenate([bk, b_v.reshape(B, T, B_KVH, B_HD)], axis=-1)
    b_logf = jax.nn.log_sigmoid(b_fl.astype(f32) + b_f.astype(f32))
    cq = rms_norm(c_q.reshape(B, T, C_HEADS, C_HD), c_qn)
    ckv = c_kvp.reshape(B, T, 6, C_KVH, C_HD)
    c_new = jnp.stack([ckv[:, :, 0], ckv[:, :, 1], rms_norm(ckv[:, :, 2], c_kn[1]), ckv[:, :, 3]], axis=3)
    w_new = jnp.stack([rms_norm(ckv[:, :, 4], c_kn[2]), ckv[:, :, 5]], axis=3)
    c_gates = jax.nn.sigmoid(c_gl.astype(f32)).reshape(B, T, C_HEADS, 3)
    Lbuf = win_buf.shape[1]
    w_all = jnp.concatenate([win_buf, w_new], axis=1)
    w_first = pos0 - Lbuf
    wpos = jnp.concatenate([jnp.full((WINDOW,), -FAR, jnp.int32),
                            w_first + jnp.arange(Lbuf + T, dtype=jnp.int32)])
    q_pos = pos0 + jnp.arange(T, dtype=jnp.int32)
    rel_a = rel_bias[:, : 2 * A_HEADS]
    rel_c = rel_bias[:, 2 * A_HEADS:]

    def per_sequence(args):
        bi, aq_b, anew_b, bq_b, bnew_b, logf_b, cq_b, cg_b, cnew_b, wall_b = args
        if past is None:
            a_kv, b_kv, logf, c_kv = anew_b, bnew_b, logf_b, cnew_b
        else:
            cache_a, cache_b, cache_lf, cache_c, table, li = past
            row = table[bi]

            def rows(cache):
                g = cache[li, row]
                return g.reshape((-1,) + g.shape[2:])
            a_kv = jnp.concatenate([rows(cache_a), anew_b], axis=0)
            b_kv = jnp.concatenate([rows(cache_b), bnew_b], axis=0)
            logf = jnp.concatenate([rows(cache_lf).astype(f32), logf_b], axis=0)
            c_kv = jnp.concatenate([rows(cache_c), cnew_b], axis=0)
        S = a_kv.shape[0]
        ak_full = a_kv[..., : 2 * A_HD].reshape(S, A_KVH, 2, A_HD)
        av_full = a_kv[..., 2 * A_HD:]
        o_a = sweep_blocks(lambda qp, q: diff_attn_block(qp, q, ak_full, av_full, lam, rel_a),
                           q_pos, (aq_b,))
        csum = jnp.cumsum(logf, axis=0)
        bk_full = b_kv[..., :B_HD]
        bv_full = b_kv[..., B_HD:]
        o_b = sweep_blocks(lambda qp, q, cqq: fox_block(qp, q, cqq, bk_full, bv_full, csum),
                           q_pos, (bq_b, csum[S - T:]))
        kc = rms_norm(compress(c_kv[:, :, 0], c_pe[0], c_w1[0], c_w2[0]), c_kn[0])
        vc = compress(c_kv[:, :, 1], c_pe[1], c_w1[1], c_w2[1])
        cmp_end = jnp.arange(kc.shape[0], dtype=jnp.int32) * L_BLK + (L_BLK - 1)
        n_sel = -(-S // L_BLK)
        sel = jnp.pad(c_kv[:, :, 2:], ((0, n_sel * L_BLK - S), (0, 0), (0, 0), (0, 0)))
        sel = sel.reshape(n_sel, L_BLK, C_KVH, 2, C_HD).transpose(3, 2, 0, 1, 4)
        wpad = jnp.pad(wall_b, ((WINDOW, 0), (0, 0), (0, 0), (0, 0)))
        o_c = sweep_blocks(lambda qp, q, g: nsa_block(qp, q, g, kc, vc, cmp_end, sel[0], sel[1],
                                                      wpad, wpos, w_first, rel_c),
                           q_pos, (cq_b, cg_b))
        return o_a, o_b, o_c

    o_a, o_b, o_c = lax.map(per_sequence, (jnp.arange(B), aq, a_new, bq, b_new, b_logf,
                                           cq, c_gates, c_new, w_all))
    o_a = rms_norm(o_a, a_subln) * (1.0 - lam_init)
    branch = jnp.stack([o_a.reshape(B, T, BRANCH_W), o_b.reshape(B, T, BRANCH_W),
                        o_c.reshape(B, T, BRANCH_W)]).astype(dt)
    branch = branch * jax.nn.silu(jnp.stack([a_g, b_g, c_g]))
    proj = jnp.einsum('nbtw,nwd->nbtd', branch, w_branch)
    merge = jax.nn.sigmoid(m_g.reshape(B, T, 3, D_MODEL))
    y = jnp.einsum('btnd,nbtd->btd', merge, proj)
    out = x + (y @ w_out).astype(dt)
    return out, a_new, b_new, b_logf, c_new, w_all[:, Lbuf + T - win_keep:]


def setup_inputs(seed: int = 0) -> dict:
    key = jax.random.key(seed)
    ks = jax.random.split(key, 32)
    f32 = jnp.float32
    n_pages = PAST_LEN // PAGE_SIZE
    n_phys = (DEC_BATCH * n_pages * 5 + 3) // 4
    win_keep = min(WINDOW, PAST_LEN)

    def nrm(k, shape, s):
        return s * jax.random.normal(k, shape, f32)

    inp = {}
    inp['x_prompt'] = nrm(ks[0], (BATCH, SEQ, D_MODEL), 1.0)
    inp['x_sample'] = nrm(ks[1], (DEC_BATCH, DEC_SEQ, D_MODEL), 1.0)
    inp['cache_a_kv'] = nrm(ks[2], (DEPTH, n_phys, PAGE_SIZE, A_KVH, 2 * A_HD + A_VD), 1.0)
    inp['cache_b_kv'] = nrm(ks[3], (DEPTH, n_phys, PAGE_SIZE, B_KVH, 2 * B_HD), 1.0)
    inp['cache_b_logf'] = jax.nn.log_sigmoid(3.0 + jax.random.normal(ks[4], (DEPTH, n_phys, PAGE_SIZE, B_HEADS), f32))
    inp['cache_c_kv'] = nrm(ks[5], (DEPTH, n_phys, PAGE_SIZE, C_KVH, 4, C_HD), 1.0)
    inp['state_c_win'] = nrm(ks[6], (DEPTH, DEC_BATCH, win_keep, C_KVH, 2, C_HD), 1.0)
    inp['page_table'] = jax.random.permutation(ks[7], n_phys)[: DEC_BATCH * n_pages].reshape(
        DEC_BATCH, n_pages).astype(jnp.int32)
    inp['rms_gain'] = 1.0 + nrm(ks[8], (DEPTH, D_MODEL), 0.05)
    inp['w_in'] = nrm(ks[9], (DEPTH, D_MODEL, N_IN), D_MODEL ** -0.5)
    inp['b_forget'] = 3.0 + nrm(ks[10], (DEPTH, B_HEADS), 0.5)
    inp['a_q_norm'] = 1.0 + nrm(ks[11], (DEPTH, A_HD), 0.05)
    inp['a_k_norm'] = 1.0 + nrm(ks[12], (DEPTH, A_HD), 0.05)
    inp['a_lambda'] = nrm(ks[13], (DEPTH, 4, A_HD), 0.1)
    inp['a_subln'] = 1.0 + nrm(ks[14], (DEPTH, A_VD), 0.05)
    inp['b_q_norm'] = 1.0 + nrm(ks[15], (DEPTH, B_HD), 0.05)
    inp['b_k_norm'] = 1.0 + nrm(ks[16], (DEPTH, B_HD), 0.05)
    inp['c_q_norm'] = 1.0 + nrm(ks[17], (DEPTH, C_HD), 0.05)
    inp['c_k_norm'] = 1.0 + nrm(ks[18], (DEPTH, 3, C_HD), 0.05)
    inp['c_cmp_pe'] = nrm(ks[19], (DEPTH, 2, L_BLK, C_HD), 0.1)
    inp['c_cmp_w1'] = nrm(ks[20], (DEPTH, 2, L_BLK * C_HD, C_HD), (L_BLK * C_HD) ** -0.5)
    inp['c_cmp_w2'] = nrm(ks[21], (DEPTH, 2, C_HD, C_HD), C_HD ** -0.5)
    inp['rel_bias'] = nrm(ks[22], (REL_BUCKETS, 2 * A_HEADS + C_HEADS), 0.5)
    inp['w_branch'] = nrm(ks[23], (DEPTH, 3, BRANCH_W, D_MODEL), BRANCH_W ** -0.5)
    inp['w_out'] = nrm(ks[24], (DEPTH, D_MODEL, D_MODEL), D_MODEL ** -0.5)
    return inp


def reference(x_prompt, x_sample, cache_a_kv, cache_b_kv, cache_b_logf, cache_c_kv, state_c_win,
              page_table, rms_gain, w_in, b_forget, a_q_norm, a_k_norm, a_lambda, a_subln,
              b_q_norm, b_k_norm, c_q_norm, c_k_norm, c_cmp_pe, c_cmp_w1, c_cmp_w2,
              rel_bias, w_branch, w_out):
    past_len = page_table.shape[1] * cache_a_kv.shape[2]
    win_keep_s = state_c_win.shape[2]
    win_keep_p = min(WINDOW, x_prompt.shape[1])
    empty_win = jnp.zeros((x_prompt.shape[0], 0, C_KVH, 2, C_HD), x_prompt.dtype)
    yp, ys = x_prompt, x_sample
    new_p, new_s = [], []
    for l in range(DEPTH):
        lam_init = 0.8 - 0.6 * math.exp(-0.3 * l)
        w = (rms_gain[l], w_in[l], b_forget[l], a_q_norm[l], a_k_norm[l], a_lambda[l], a_subln[l],
             b_q_norm[l], b_k_norm[l], c_q_norm[l], c_k_norm[l], c_cmp_pe[l], c_cmp_w1[l], c_cmp_w2[l],
             rel_bias, w_branch[l], w_out[l])
        yp, *st_p = mixer_layer(yp, 0, empty_win, win_keep_p, None, lam_init, *w)
        ys, *st_s = mixer_layer(ys, past_len, state_c_win[l], win_keep_s,
                                (cache_a_kv, cache_b_kv, cache_b_logf, cache_c_kv, page_table, l),
                                lam_init, *w)
        new_p.append(st_p)
        new_s.append(st_s)

    def stk(lst, i):
        return jnp.stack([s[i] for s in lst])

    return (yp, ys, stk(new_p, 0), stk(new_s, 0), stk(new_p, 1), stk(new_s, 1),
            stk(new_p, 2), stk(new_s, 2), stk(new_p, 3), stk(new_s, 3), stk(new_p, 4), stk(new_s, 4))
```

```python
import functools
import math

import numpy as np
import jax
import jax.numpy as jnp
from jax import lax
from jax.experimental import pallas as pl
from jax.experimental.pallas import tpu as pltpu

F32 = jnp.float32
BF16 = jnp.bfloat16

A_HD, A_VD, A_HEADS, A_KVH = 64, 128, 8, 4
B_HD, B_HEADS, B_KVH = 128, 8, 4
C_HD, C_HEADS, C_KVH, C_GRP = 128, 8, 2, 4
L_BLK, N_SEL, WINDOW = 64, 16, 512
REL_BUCKETS, REL_MAX_DIST = 32, 128
EPS, NEG, FORCE = 1e-6, -1e30, 1e6
SCALE_A, SCALE_BC = A_HD ** -0.5, 128 ** -0.5
LANE = 128
VMEM_MB = 48

_O = np.cumsum([0, 1024, 512, 512, 1024, 1024, 512, 512, 8, 1024, 1024, 1536, 24, 1024, 6144])
_PERM = [(9760, 15904), (0, 1024), (2048, 3072), (3072, 4096), (5128, 6152), (6152, 7176), (8736, 9760),
         (1024, 1536), (1536, 2048), (4096, 4608), (4608, 5120), (7176, 8712), (5120, 5128), (8712, 8736)]
N_USED = 16000
N_PAD = 16128
TN = 768
BLK_AQ, BLK_AG, BLK_BQ, BLK_BG, BLK_CQ, BLK_CG = 6, 7, 8, 9, 10, 11
BLK_AK, BLK_AV, BLK_BK, BLK_BV, BLK_C0, BLK_C1, BLK_C2 = 24, 25, 26, 27, 28, 29, 30
BLK_SM = 124


def _cp(sem, mb=VMEM_MB):
    return pltpu.CompilerParams(dimension_semantics=sem, vmem_limit_bytes=mb * 1024 * 1024)


def _smem():
    return pl.BlockSpec(memory_space=pltpu.SMEM)


def _rms(x, g):
    ms = jnp.mean(x * x, axis=-1, keepdims=True)
    return x * lax.rsqrt(ms + EPS) * g


def _rms_half(x, g2):
    sq = x * x
    lo = lax.broadcasted_iota(jnp.int32, x.shape, 1) < 64
    s_lo = jnp.sum(jnp.where(lo, sq, 0.0), axis=-1, keepdims=True)
    s_hi = jnp.sum(jnp.where(lo, 0.0, sq), axis=-1, keepdims=True)
    ms = jnp.where(lo, s_lo, s_hi) * (1.0 / 64)
    return x * lax.rsqrt(ms + EPS) * g2


def _sigmoid(x):
    return 1.0 / (1.0 + jnp.exp(-x))


def _dot_t(a, b):
    return lax.dot_general(a, b, (((1,), (1,)), ((), ())), preferred_element_type=F32)


def _fa_update(s, v, m_ref, l_ref, acc_ref, idx):
    m_prev = m_ref[idx]
    m_new = jnp.maximum(m_prev, jnp.max(s, axis=-1, keepdims=True))
    alpha = jnp.exp(m_prev - m_new)
    p = jnp.exp(s - m_new)
    l_ref[idx] = alpha * l_ref[idx] + jnp.sum(p, axis=-1, keepdims=True)
    acc_ref[idx] = alpha * acc_ref[idx] + jnp.dot(p.astype(BF16), v, preferred_element_type=F32)
    m_ref[idx] = m_new


def _fa_init(m_ref, l_ref, acc_ref):
    m_ref[...] = jnp.full(m_ref.shape, NEG, F32)
    l_ref[...] = jnp.zeros(l_ref.shape, F32)
    acc_ref[...] = jnp.zeros(acc_ref.shape, F32)


def _proj_kernel(x_ref, g_ref, w_ref, o_ref, h_ref):
    @pl.when(pl.program_id(1) == 0)
    def _():
        h_ref[...] = _rms(x_ref[...], g_ref[...]).astype(BF16)

    o_ref[...] = jnp.dot(h_ref[...], w_ref[...], preferred_element_type=F32)


def _proj(x, g, w):
    m, d = x.shape
    tm = next(c for c in (1024, 512, 256, 128, m) if m % c == 0)
    return pl.pallas_call(
        _proj_kernel,
        grid=(m // tm, N_PAD // TN),
        in_specs=[pl.BlockSpec((tm, d), lambda i, j: (i, 0)),
                  pl.BlockSpec((1, d), lambda i, j: (0, 0)),
                  pl.BlockSpec((d, TN), lambda i, j: (0, j))],
        out_specs=pl.BlockSpec((tm, TN), lambda i, j: (i, j)),
        out_shape=jax.ShapeDtypeStruct((m, N_PAD), F32),
        scratch_shapes=[pltpu.VMEM((tm, d), BF16)],
        compiler_params=_cp(("parallel", "arbitrary")),
    )(x, g.reshape(1, d), w)


def _post_kernel(zaq, zbq, zcq, zak, zav, zbk, zbv, zc0, zc1, zc2, zsm, pn, bf,
                 qa, qb, qc, anew, bnew, cnew, wnew, sm):
    for h in range(8):
        sl = slice(h * LANE, (h + 1) * LANE)
        qa[:, sl] = (_rms_half(zaq[:, sl], pn[0:1]) * SCALE_A).astype(BF16)
        qb[:, sl] = (_rms(zbq[:, sl], pn[2:3]) * SCALE_BC).astype(BF16)
        qc[:, sl] = (_rms(zcq[:, sl], pn[4:5]) * SCALE_BC).astype(BF16)
    for k in range(4):
        src = slice(k * LANE, (k + 1) * LANE)
        anew[:, k * 256:k * 256 + 128] = _rms_half(zak[:, src], pn[1:2])
        anew[:, k * 256 + 128:(k + 1) * 256] = zav[:, src]
        bnew[:, k * 256:k * 256 + 128] = _rms(zbk[:, src], pn[3:4])
        bnew[:, k * 256 + 128:(k + 1) * 256] = zbv[:, src]
    for k in range(2):
        s0 = slice(k * LANE, (k + 1) * LANE)
        s1 = slice(256 + k * LANE, 256 + (k + 1) * LANE)
        cnew[:, k * 512:k * 512 + 128] = zc0[:, s0]
        cnew[:, k * 512 + 128:k * 512 + 256] = zc0[:, s1]
        cnew[:, k * 512 + 256:k * 512 + 384] = _rms(zc1[:, s0], pn[6:7])
        cnew[:, k * 512 + 384:k * 512 + 512] = zc1[:, s1]
        wnew[:, k * 256:k * 256 + 128] = _rms(zc2[:, s0], pn[7:8])
        wnew[:, k * 256 + 128:k * 256 + 256] = zc2[:, s1]
    z = zsm[...]
    x = z + bf[...]
    logf = jnp.minimum(x, 0.0) - jnp.log1p(jnp.exp(-jnp.abs(x)))
    lane = lax.broadcasted_iota(jnp.int32, z.shape, 1)
    sm[...] = jnp.where(lane < 8, logf, _sigmoid(z))


def _post(z, pn, bf):
    m = z.shape[0]
    tm = min(m, 256)

    def zs(width, blk):
        return pl.BlockSpec((tm, width), lambda i, blk=blk: (i, blk))

    def os(width):
        return pl.BlockSpec((tm, width), lambda i: (i, 0))

    const = lambda a: pl.BlockSpec(a.shape, lambda i: (0, 0))
    return pl.pallas_call(
        _post_kernel,
        grid=(m // tm,),
        in_specs=[zs(1024, BLK_AQ), zs(1024, BLK_BQ), zs(1024, BLK_CQ), zs(512, BLK_AK), zs(512, BLK_AV),
                  zs(512, BLK_BK), zs(512, BLK_BV), zs(512, BLK_C0), zs(512, BLK_C1), zs(512, BLK_C2),
                  zs(128, BLK_SM), const(pn), const(bf)],
        out_specs=[os(1024), os(1024), os(1024), os(1024), os(1024), os(1024), os(512), os(128)],
        out_shape=[jax.ShapeDtypeStruct((m, 1024), BF16)] * 3
        + [jax.ShapeDtypeStruct((m, 1024), F32)] * 3
        + [jax.ShapeDtypeStruct((m, 512), F32), jax.ShapeDtypeStruct((m, 128), F32)],
        compiler_params=_cp(("parallel",)),
    )(*([z] * 11), pn, bf)


def _attn_a_kernel(cfar_ref, lam_ref, q_ref, kv_ref, tab_ref, o_ref, m_ref, l_ref, acc_ref, *, tq, tk):
    kh = pl.program_id(1)
    i = pl.program_id(2)
    q = q_ref[...]
    lo = lax.broadcasted_iota(jnp.int32, (tq, LANE), 1) < 64
    zero = jnp.zeros((tq, LANE), BF16)
    qb = []
    for g in range(2):
        qh = q[:, g * LANE:(g + 1) * LANE]
        qb.append(jnp.where(lo, qh, zero))
        qb.append(jnp.where(lo, zero, qh))
    _fa_init(m_ref, l_ref, acc_ref)

    def tile(kt, bias):
        r0 = pl.multiple_of(kt * tk, tk)
        k = kv_ref[pl.ds(r0, tk), 0:128].astype(BF16)
        v = kv_ref[pl.ds(r0, tk), 128:256].astype(BF16)
        for b in range(4):
            _fa_update(_dot_t(qb[b], k) + bias(b), v, m_ref, l_ref, acc_ref, b)

    def far(kt, c):
        tile(kt, lambda b: cfar_ref[kh * 4 + b])
        return c

    lax.fori_loop(0, jnp.maximum(i - 1, 0), far, 0)

    @pl.when(i >= 1)
    def _():
        tile(i - 1, lambda b: tab_ref[1, b])

    tile(i, lambda b: tab_ref[0, b])
    lam = lam_ref[0]
    for g in range(2):
        o1 = acc_ref[2 * g] / l_ref[2 * g]
        o2 = acc_ref[2 * g + 1] / l_ref[2 * g + 1]
        o_ref[:, g * LANE:(g + 1) * LANE] = o1 - lam * o2


def _attn_a(qa, a_new, tabs, cfar, lam, tq):
    bsz, t, _ = qa.shape
    return pl.pallas_call(
        functools.partial(_attn_a_kernel, tq=tq, tk=tq),
        grid=(bsz, A_KVH, t // tq),
        in_specs=[_smem(), _smem(),
                  pl.BlockSpec((None, tq, 256), lambda b, k, i: (b, i, k)),
                  pl.BlockSpec((None, t, 256), lambda b, k, i: (b, 0, k)),
                  pl.BlockSpec((None, 2, 4, tq, tq), lambda b, k, i: (k, 0, 0, 0, 0))],
        out_specs=pl.BlockSpec((None, tq, 256), lambda b, k, i: (b, i, k)),
        out_shape=jax.ShapeDtypeStruct((bsz, t, 1024), F32),
        scratch_shapes=[pltpu.VMEM((4, tq, 1), F32), pltpu.VMEM((4, tq, 1), F32), pltpu.VMEM((4, tq, 128), F32)],
        compiler_params=_cp(("parallel", "parallel", "arbitrary")),
    )(cfar, lam, qa, a_new, tabs)


def _attn_b_kernel(q_ref, kv_ref, cq_ref, ck_ref, tri_ref, o_ref, m_ref, l_ref, acc_ref, *, tq, tk):
    i = pl.program_id(2)
    q = q_ref[...]
    _fa_init(m_ref, l_ref, acc_ref)

    def tile(kt, diag):
        r0 = pl.multiple_of(kt * tk, tk)
        k = kv_ref[pl.ds(r0, tk), 0:128].astype(BF16)
        v = kv_ref[pl.ds(r0, tk), 128:256].astype(BF16)
        for g in range(2):
            decay = cq_ref[:, g:g + 1] - ck_ref[g:g + 1, pl.ds(r0, tk)]
            s = _dot_t(q[:, g * LANE:(g + 1) * LANE], k) + decay
            if diag:
                s = s + tri_ref[...]
            _fa_update(s, v, m_ref, l_ref, acc_ref, g)

    def far(kt, c):
        tile(kt, False)
        return c

    lax.fori_loop(0, i, far, 0)
    tile(i, True)
    for g in range(2):
        o_ref[:, g * LANE:(g + 1) * LANE] = acc_ref[g] / l_ref[g]


def _attn_b(qb, b_new, csq, cst, tri, tq):
    bsz, t, _ = qb.shape
    return pl.pallas_call(
        functools.partial(_attn_b_kernel, tq=tq, tk=tq),
        grid=(bsz, B_KVH, t // tq),
        in_specs=[pl.BlockSpec((None, tq, 256), lambda b, k, i: (b, i, k)),
                  pl.BlockSpec((None, t, 256), lambda b, k, i: (b, 0, k)),
                  pl.BlockSpec((None, None, tq, 2), lambda b, k, i: (b, k, i, 0)),
                  pl.BlockSpec((None, None, 2, t), lambda b, k, i: (b, k, 0, 0)),
                  pl.BlockSpec((tq, tq), lambda b, k, i: (0, 0))],
        out_specs=pl.BlockSpec((None, tq, 256), lambda b, k, i: (b, i, k)),
        out_shape=jax.ShapeDtypeStruct((bsz, t, 1024), F32),
        scratch_shapes=[pltpu.VMEM((2, tq, 1), F32), pltpu.VMEM((2, tq, 1), F32), pltpu.VMEM((2, tq, 128), F32)],
        compiler_params=_cp(("parallel", "parallel", "arbitrary")),
    )(qb, b_new, csq, cst, tri)


def _compress(x_refs, flat_ref, pe_ref, w1_ref, w2_ref, kn_ref, n_cmp):
    outs = []
    for c in range(2):
        for l in range(L_BLK):
            rows = x_refs[c][pl.ds(l, n_cmp, stride=L_BLK), :]
            flat_ref[:, l * LANE:(l + 1) * LANE] = (rows + pe_ref[c, l:l + 1, :]).astype(BF16)
        h = jnp.dot(flat_ref[...], w1_ref[c], preferred_element_type=F32)
        h = h * _sigmoid(h)
        o = jnp.dot(h.astype(BF16), w2_ref[c], preferred_element_type=F32)
        if c == 0:
            o = _rms(o, kn_ref[...])
        outs.append(o)
    return outs


def _cmp_and_select(qg, kc, vc, bias_of, q0, rows, nsp, n_sel):
    ncp = kc.shape[0]
    imp = jnp.zeros((rows, ncp), F32)
    ocmp = []
    for g, q in enumerate(qg):
        bias = bias_of(g)
        s = _dot_t(q, kc) + bias
        e = jnp.exp(s - jnp.max(s, axis=-1, keepdims=True))
        pc = jnp.where(bias > 0.5 * NEG, e / jnp.sum(e, axis=-1, keepdims=True), 0.0)
        imp = imp + pc
        ocmp.append(jnp.dot(pc.astype(BF16), vc, preferred_element_type=F32))
    if nsp > ncp:
        imp = jnp.concatenate([imp, jnp.zeros((rows, nsp - ncp), F32)], axis=1)
    j = lax.broadcasted_iota(jnp.int32, (rows, nsp), 1)
    pos = q0 + lax.broadcasted_iota(jnp.int32, (rows, nsp), 0)
    cur = lax.shift_right_logical(pos, 6)
    imp = imp + FORCE * (jnp.where(j == cur, 4.0, 0.0) + jnp.where(j == cur - 1, 2.0, 0.0)
                         + jnp.where(j == 0, 1.0, 0.0))
    imp = jnp.where(j <= cur, imp, NEG)
    cnt = jnp.zeros((rows, nsp), F32)
    for c in range(n_sel):
        col = imp[:, c:c + 1]
        first = jnp.where(j > c, 1.0, 0.0)
        cnt = cnt + jnp.where(col > imp, 1.0, jnp.where(col == imp, first, 0.0))
    sel = jnp.where(cnt < float(min(N_SEL, n_sel)), 1.0, 0.0)
    return ocmp, sel


def _cmp_p_kernel(xk_ref, xv_ref, pe_ref, w1_ref, w2_ref, kn_ref, kc_ref, vc_ref, flat_ref, *, n_cmp):
    kc, vc = _compress((xk_ref, xv_ref), flat_ref, pe_ref, w1_ref, w2_ref, kn_ref, n_cmp)
    kc_ref[...] = jnp.zeros(kc_ref.shape, BF16)
    vc_ref[...] = jnp.zeros(vc_ref.shape, BF16)
    kc_ref[0:n_cmp, :] = kc.astype(BF16)
    vc_ref[0:n_cmp, :] = vc.astype(BF16)


def _cmp_p(c_new, pe, w1, w2, kn0, ncp):
    bsz, t, _ = c_new.shape
    n_cmp = t // L_BLK
    whole = lambda a: pl.BlockSpec(a.shape, lambda b, k: (0,) * a.ndim)
    return pl.pallas_call(
        functools.partial(_cmp_p_kernel, n_cmp=n_cmp),
        grid=(bsz, C_KVH),
        in_specs=[pl.BlockSpec((None, t, 128), lambda b, k: (b, 0, 4 * k)),
                  pl.BlockSpec((None, t, 128), lambda b, k: (b, 0, 4 * k + 1)),
                  whole(pe), whole(w1), whole(w2), whole(kn0)],
        out_specs=[pl.BlockSpec((None, None, ncp, 128), lambda b, k: (b, k, 0, 0))] * 2,
        out_shape=[jax.ShapeDtypeStruct((bsz, C_KVH, ncp, 128), BF16)] * 2,
        scratch_shapes=[pltpu.VMEM((n_cmp, L_BLK * C_HD), BF16)],
        compiler_params=_cp(("parallel", "parallel")),
    )(c_new, c_new, pe, w1, w2, kn0)


def _nsa_p_kernel(cfar_ref, q_ref, kc_ref, vc_ref, sel_ref, win_ref, gt_ref, bc_ref, tab_ref, e_ref, o_ref,
                  m_ref, l_ref, acc_ref, am_ref, *, tq, tk, n_sel, nsp, nw):
    kh = pl.program_id(1)
    i = pl.program_id(2)
    q = q_ref[...]
    qg = [q[:, g * LANE:(g + 1) * LANE] for g in range(C_GRP)]
    ocmp, sel = _cmp_and_select(qg, kc_ref[...], vc_ref[...], lambda g: bc_ref[g], i * tq, tq, nsp, n_sel)
    am_ref[...] = (jnp.dot(sel.astype(BF16), e_ref[...], preferred_element_type=F32) - 1.0) * (-NEG)
    _fa_init(m_ref, l_ref, acc_ref)

    def tile(src_ref, kt, bias, masked, slot0):
        r0 = pl.multiple_of(kt * tk, tk)
        k = src_ref[pl.ds(r0, tk), 0:128].astype(BF16)
        v = src_ref[pl.ds(r0, tk), 128:256].astype(BF16)
        for g in range(C_GRP):
            s = _dot_t(qg[g], k) + bias(g)
            if masked:
                s = s + am_ref[:, pl.ds(r0, tk)]
            _fa_update(s, v, m_ref, l_ref, acc_ref, slot0 + g)

    far_bias = lambda g: cfar_ref[kh * C_GRP + g]

    def far(kt, c):
        tile(sel_ref, kt, far_bias, True, 0)
        return c

    lax.fori_loop(0, jnp.maximum(i - 1, 0), far, 0)

    @pl.when(i >= 1)
    def _():
        tile(sel_ref, i - 1, lambda g: tab_ref[1, g], True, 0)
        tile(win_ref, i - 1, lambda g: tab_ref[1, g], False, C_GRP)

    tile(sel_ref, i, lambda g: tab_ref[0, g], True, 0)
    tile(win_ref, i, lambda g: tab_ref[0, g], False, C_GRP)
    for d in range(2, nw):
        @pl.when(i >= d)
        def _(d=d):
            tile(win_ref, i - d, far_bias, False, C_GRP)

    @pl.when(i >= nw)
    def _():
        tile(win_ref, i - nw, lambda g: tab_ref[2, g], False, C_GRP)

    gt = gt_ref[...]
    for g in range(C_GRP):
        o_sel = acc_ref[g] / l_ref[g]
        o_win = acc_ref[C_GRP + g] / l_ref[C_GRP + g]
        o_ref[:, g * LANE:(g + 1) * LANE] = (gt[:, 3 * g:3 * g + 1] * ocmp[g] + gt[:, 3 * g + 1:3 * g + 2] * o_sel
                                            + gt[:, 3 * g + 2:3 * g + 3] * o_win)


def _nsa_p(qc, kc, vc, c_new, w_new, gates, bias_cmp, tabs, expand, cfar, tq):
    bsz, t, _ = qc.shape
    ncp = kc.shape[2]
    n_sel = -(-t // L_BLK)
    nsp = expand.shape[0]
    kern = functools.partial(_nsa_p_kernel, tq=tq, tk=tq, n_sel=n_sel, nsp=nsp, nw=WINDOW // tq)
    return pl.pallas_call(
        kern,
        grid=(bsz, C_KVH, t // tq),
        in_specs=[_smem(),
                  pl.BlockSpec((None, tq, 512), lambda b, k, i: (b, i, k)),
                  pl.BlockSpec((None, None, ncp, 128), lambda b, k, i: (b, k, 0, 0)),
                  pl.BlockSpec((None, None, ncp, 128), lambda b, k, i: (b, k, 0, 0)),
                  pl.BlockSpec((None, t, 256), lambda b, k, i: (b, 0, 2 * k + 1)),
                  pl.BlockSpec((None, t, 256), lambda b, k, i: (b, 0, k)),
                  pl.BlockSpec((None, None, tq, 12), lambda b, k, i: (b, k, i, 0)),
                  pl.BlockSpec((C_GRP, tq, ncp), lambda b, k, i: (k, i, 0)),
                  pl.BlockSpec((None, 3, C_GRP, tq, tq), lambda b, k, i: (k, 0, 0, 0, 0)),
                  pl.BlockSpec(expand.shape, lambda b, k, i: (0, 0))],
        out_specs=pl.BlockSpec((None, tq, 512), lambda b, k, i: (b, i, k)),
        out_shape=jax.ShapeDtypeStruct((bsz, t, 1024), F32),
        scratch_shapes=[pltpu.VMEM((2 * C_GRP, tq, 1), F32), pltpu.VMEM((2 * C_GRP, tq, 1), F32),
                        pltpu.VMEM((2 * C_GRP, tq, 128), F32), pltpu.VMEM((tq, t), F32)],
        compiler_params=_cp(("parallel", "parallel", "arbitrary")),
    )(cfar, qc, kc, vc, c_new, w_new, gates, bias_cmp, tabs, expand)


def _pad_rows(x, n):
    return jnp.concatenate([x, jnp.zeros((n - x.shape[0], x.shape[1]), x.dtype)], axis=0)


def _paged_update(q, ks, vs, biases, m_ref, l_ref, acc_ref, idx):
    s_all = jnp.concatenate([_dot_t(q, k) + b for k, b in zip(ks, biases)], axis=1)
    m_prev = m_ref[idx]
    m_new = jnp.maximum(m_prev, jnp.max(s_all, axis=-1, keepdims=True))
    alpha = jnp.exp(m_prev - m_new)
    p = jnp.exp(s_all - m_new)
    l_ref[idx] = alpha * l_ref[idx] + jnp.sum(p, axis=-1, keepdims=True)
    acc = alpha * acc_ref[idx]
    for u, v in enumerate(vs):
        acc = acc + jnp.dot(p[:, u * LANE:(u + 1) * LANE].astype(BF16), v, preferred_element_type=F32)
    acc_ref[idx] = acc
    m_ref[idx] = m_new


def _attn_a_s_kernel(pt_ref, lam_ref, q_ref, *rest, pps, n_steps):
    caches = rest[:pps]
    new_ref, blast_ref, bnew_ref, cfar_ref, o_ref, m_ref, l_ref, acc_ref = rest[pps:]
    j = pl.program_id(1)
    last = j == n_steps - 1

    @pl.when(j == 0)
    def _():
        _fa_init(m_ref, l_ref, acc_ref)

    for kh in range(A_KVH):
        q = q_ref[kh]
        ks = [c[:, kh * 256:kh * 256 + 128].astype(BF16) for c in caches]
        vs = [c[:, kh * 256 + 128:(kh + 1) * 256].astype(BF16) for c in caches]
        far = cfar_ref[kh]
        biases = [far] * (pps - 1) + [jnp.where(last, blast_ref[kh], far)]
        _paged_update(q, ks, vs, biases, m_ref, l_ref, acc_ref, kh)

    @pl.when(last)
    def _():
        lam = lam_ref[0]
        for kh in range(A_KVH):
            kn = _pad_rows(new_ref[:, kh * 256:kh * 256 + 128], LANE).astype(BF16)
            vn = _pad_rows(new_ref[:, kh * 256 + 128:(kh + 1) * 256], LANE).astype(BF16)
            _paged_update(q_ref[kh], [kn], [vn], [bnew_ref[kh]], m_ref, l_ref, acc_ref, kh)
            o = acc_ref[kh] / l_ref[kh]
            for g in range(2):
                o_ref[kh, g * 8:(g + 1) * 8, :] = o[(2 * g) * 8:(2 * g + 1) * 8] - lam * o[(2 * g + 1) * 8:(2 * g + 2) * 8]


def _page_specs(layer, pps, width, lane_blk):
    return [pl.BlockSpec((None, None, 128, width),
                         lambda s, j, pt, u=u: (layer, pt[s, j * pps + u], 0, lane_blk))
            for u in range(pps)]


def _attn_a_s(pt, lam, qs, cache, layer, new, blast, bnew, cfar, pps):
    n_seq, n_pages = pt.shape
    n_steps = n_pages // pps
    c3 = lambda a: pl.BlockSpec(a.shape, lambda s, j, pt: (0, 0, 0))
    grid_spec = pltpu.PrefetchScalarGridSpec(
        num_scalar_prefetch=1,
        grid=(n_seq, n_steps),
        in_specs=[_smem(), pl.BlockSpec((None, 4, 32, 128), lambda s, j, pt: (s, 0, 0, 0))]
        + _page_specs(layer, pps, 1024, 0)
        + [pl.BlockSpec((None, 8, 1024), lambda s, j, pt: (s, 0, 0)), c3(blast), c3(bnew), c3(cfar)],
        out_specs=pl.BlockSpec((None, 4, 16, 128), lambda s, j, pt: (s, 0, 0, 0)),
        scratch_shapes=[pltpu.VMEM((4, 32, 1), F32), pltpu.VMEM((4, 32, 1), F32), pltpu.VMEM((4, 32, 128), F32)],
    )
    return pl.pallas_call(
        functools.partial(_attn_a_s_kernel, pps=pps, n_steps=n_steps),
        grid_spec=grid_spec,
        out_shape=jax.ShapeDtypeStruct((n_seq, 4, 16, 128), F32),
        compiler_params=_cp(("parallel", "arbitrary")),
    )(pt, lam, qs, *([cache] * pps), new, blast, bnew, cfar)


def _attn_b_s_kernel(pt_ref, q_ref, cq_ref, *rest, pps, n_steps):
    caches = rest[:pps]
    ck_ref, ckn_ref, new_ref, tri_ref, o_ref, m_ref, l_ref, acc_ref = rest[pps:]
    j = pl.program_id(1)

    @pl.when(j == 0)
    def _():
        _fa_init(m_ref, l_ref, acc_ref)

    def decay(ck, kh, lo):
        rows = [jnp.broadcast_to(ck[kh, g:g + 1, lo:lo + LANE], (8, LANE)) for g in range(2)]
        return cq_ref[kh] - jnp.concatenate(rows, axis=0)

    for kh in range(B_KVH):
        ks = [c[:, kh * 256:kh * 256 + 128].astype(BF16) for c in caches]
        vs = [c[:, kh * 256 + 128:(kh + 1) * 256].astype(BF16) for c in caches]
        biases = [decay(ck_ref, kh, u * LANE) for u in range(pps)]
        _paged_update(q_ref[kh], ks, vs, biases, m_ref, l_ref, acc_ref, kh)

    @pl.when(j == n_steps - 1)
    def _():
        for kh in range(B_KVH):
            kn = _pad_rows(new_ref[:, kh * 256:kh * 256 + 128], LANE).astype(BF16)
            vn = _pad_rows(new_ref[:, kh * 256 + 128:(kh + 1) * 256], LANE).astype(BF16)
            _paged_update(q_ref[kh], [kn], [vn], [decay(ckn_ref, kh, 0) + tri_ref[...]], m_ref, l_ref, acc_ref, kh)
            o_ref[kh] = acc_ref[kh] / l_ref[kh]


def _attn_b_s(pt, qs, cqb, cache, layer, cst, new, tri, pps):
    n_seq, n_pages = pt.shape
    n_steps = n_pages // pps
    grid_spec = pltpu.PrefetchScalarGridSpec(
        num_scalar_prefetch=1,
        grid=(n_seq, n_steps),
        in_specs=[pl.BlockSpec((None, 4, 16, 128), lambda s, j, pt: (s, 0, 0, 0)),
                  pl.BlockSpec((None, 4, 16, 128), lambda s, j, pt: (s, 0, 0, 0))]
        + _page_specs(layer, pps, 1024, 0)
        + [pl.BlockSpec((None, 4, 2, pps * 128), lambda s, j, pt: (s, 0, 0, j)),
           pl.BlockSpec((None, 4, 2, 128), lambda s, j, pt: (s, 0, 0, n_pages)),
           pl.BlockSpec((None, 8, 1024), lambda s, j, pt: (s, 0, 0)),
           pl.BlockSpec(tri.shape, lambda s, j, pt: (0, 0))],
        out_specs=pl.BlockSpec((None, 4, 16, 128), lambda s, j, pt: (s, 0, 0, 0)),
        scratch_shapes=[pltpu.VMEM((4, 16, 1), F32), pltpu.VMEM((4, 16, 1), F32), pltpu.VMEM((4, 16, 128), F32)],
    )
    return pl.pallas_call(
        functools.partial(_attn_b_s_kernel, pps=pps, n_steps=n_steps),
        grid_spec=grid_spec,
        out_shape=jax.ShapeDtypeStruct((n_seq, 4, 16, 128), F32),
        compiler_params=_cp(("parallel", "arbitrary")),
    )(pt, qs, cqb, *([cache] * pps), cst, cst, new, tri)


def _nsa_s1_kernel(pt_ref, q_ref, *rest, pps, n_steps, n_cmp, n_sel, nsp, q0):
    caches = rest[:pps]
    pe_ref, w1_ref, w2_ref, kn_ref, bc_ref, ocmp_ref, sel_ref, xk_ref, xv_ref, flat_ref = rest[pps:]
    j = pl.program_id(2)
    for u, c in enumerate(caches):
        r0 = pl.multiple_of((j * pps + u) * 128, 128)
        xk_ref[pl.ds(r0, 128), :] = c[:, 0:128]
        xv_ref[pl.ds(r0, 128), :] = c[:, 128:256]

    @pl.when(j == n_steps - 1)
    def _():
        kc, vc = _compress((xk_ref, xv_ref), flat_ref, pe_ref, w1_ref, w2_ref, kn_ref, n_cmp)
        q = q_ref[...]
        qg = [q[g * 8:(g + 1) * 8] for g in range(C_GRP)]
        ocmp, sel = _cmp_and_select(qg, kc.astype(BF16), vc.astype(BF16),
                                    lambda g: bc_ref[g * 8:(g + 1) * 8], q0, 8, nsp, n_sel)
        for g in range(C_GRP):
            ocmp_ref[g * 8:(g + 1) * 8, :] = ocmp[g]
        sel_ref[...] = sel


def _nsa_s1(pt, qs, cache, layer, pe, w1, w2, kn0, bias_cmp, pps, n_sel, nsp, q0):
    n_seq, n_pages = pt.shape
    n_steps = n_pages // pps
    n_cmp = n_pages * 128 // L_BLK
    whole = lambda a: pl.BlockSpec(a.shape, lambda s, k, j, pt: (0,) * a.ndim)
    pages = [pl.BlockSpec((None, None, 128, 256),
                          lambda s, k, j, pt, u=u: (layer, pt[s, j * pps + u], 0, 2 * k)) for u in range(pps)]
    grid_spec = pltpu.PrefetchScalarGridSpec(
        num_scalar_prefetch=1,
        grid=(n_seq, C_KVH, n_steps),
        in_specs=[pl.BlockSpec((None, None, 32, 128), lambda s, k, j, pt: (s, k, 0, 0))] + pages
        + [whole(pe), whole(w1), whole(w2), whole(kn0),
           pl.BlockSpec((None, 32, n_cmp), lambda s, k, j, pt: (k, 0, 0))],
        out_specs=[pl.BlockSpec((None, None, 32, 128), lambda s, k, j, pt: (s, k, 0, 0)),
                   pl.BlockSpec((None, None, 8, nsp), lambda s, k, j, pt: (s, k, 0, 0))],
        scratch_shapes=[pltpu.VMEM((n_pages * 128, 128), F32), pltpu.VMEM((n_pages * 128, 128), F32),
                        pltpu.VMEM((n_cmp, L_BLK * C_HD), BF16)],
    )
    return pl.pallas_call(
        functools.partial(_nsa_s1_kernel, pps=pps, n_steps=n_steps, n_cmp=n_cmp, n_sel=n_sel, nsp=nsp, q0=q0),
        grid_spec=grid_spec,
        out_shape=[jax.ShapeDtypeStruct((n_seq, C_KVH, 32, 128), F32),
                   jax.ShapeDtypeStruct((n_seq, C_KVH, 8, nsp), F32)],
        compiler_params=_cp(("parallel", "parallel", "arbitrary")),
    )(pt, qs, *([cache] * pps), pe, w1, w2, kn0, bias_cmp)


def _nsa_s2_kernel(pt_ref, q_ref, *rest, pps, n_steps):
    caches = rest[:pps]
    (am_ref, amn_ref, new_ref, wst_ref, wnew_ref, blast_ref, bnew_ref, cfar_ref, bwin_ref, gt_ref, ocmp_ref,
     o_ref, m_ref, l_ref, acc_ref) = rest[pps:]
    j = pl.program_id(2)
    last = j == n_steps - 1
    q = q_ref[...]

    @pl.when(j == 0)
    def _():
        _fa_init(m_ref, l_ref, acc_ref)

    tile4 = lambda a: jnp.concatenate([a] * C_GRP, axis=0)
    ks = [c[:, 0:128].astype(BF16) for c in caches]
    vs = [c[:, 128:256].astype(BF16) for c in caches]
    far = cfar_ref[...]
    biases = [far + tile4(am_ref[:, u * LANE:(u + 1) * LANE]) for u in range(pps - 1)]
    biases.append(jnp.where(last, blast_ref[...], far) + tile4(am_ref[:, (pps - 1) * LANE:pps * LANE]))
    _paged_update(q, ks, vs, biases, m_ref, l_ref, acc_ref, 0)

    @pl.when(last)
    def _():
        kn = _pad_rows(new_ref[:, 0:128], LANE).astype(BF16)
        vn = _pad_rows(new_ref[:, 128:256], LANE).astype(BF16)
        _paged_update(q, [kn], [vn], [bnew_ref[...] + tile4(amn_ref[...])], m_ref, l_ref, acc_ref, 0)
        o_sel = acc_ref[0] / l_ref[0]
        wk = wst_ref[:, 0:128].astype(BF16)
        wv = wst_ref[:, 128:256].astype(BF16)
        wkn = _pad_rows(wnew_ref[:, 0:128], LANE).astype(BF16)
        wvn = _pad_rows(wnew_ref[:, 128:256], LANE).astype(BF16)
        nst = wk.shape[0]
        s = jnp.concatenate([_dot_t(q, wk) + bwin_ref[:, 0:nst], _dot_t(q, wkn) + bnew_ref[...]], axis=1)
        p = jnp.exp(s - jnp.max(s, axis=-1, keepdims=True))
        o_win = (jnp.dot(p[:, 0:nst].astype(BF16), wv, preferred_element_type=F32)
                 + jnp.dot(p[:, nst:].astype(BF16), wvn, preferred_element_type=F32)) / jnp.sum(p, axis=-1, keepdims=True)
        gt = gt_ref[...]
        o_ref[...] = gt[:, 0:128] * ocmp_ref[...] + gt[:, 128:256] * o_sel + gt[:, 256:384] * o_win


def _nsa_s2(pt, qs, cache, layer, addmask, c_new8, win_state, w_new8, blast, bnew, cfar, bwin, gates, ocmp, pps):
    n_seq, n_pages = pt.shape
    n_steps = n_pages // pps
    nst = win_state.shape[2]
    per_k = lambda a: pl.BlockSpec((None,) + a.shape[1:], lambda s, k, j, pt: (k, 0, 0))
    per_sk = lambda r, c: pl.BlockSpec((None, None, r, c), lambda s, k, j, pt: (s, k, 0, 0))
    pages = [pl.BlockSpec((None, None, 128, 256),
                          lambda s, k, j, pt, u=u: (layer, pt[s, j * pps + u], 0, 2 * k + 1)) for u in range(pps)]
    grid_spec = pltpu.PrefetchScalarGridSpec(
        num_scalar_prefetch=1,
        grid=(n_seq, C_KVH, n_steps),
        in_specs=[per_sk(32, 128)] + pages
        + [pl.BlockSpec((None, None, 8, pps * 128), lambda s, k, j, pt: (s, k, 0, j)),
           pl.BlockSpec((None, None, 8, 128), lambda s, k, j, pt: (s, k, 0, n_pages)),
           pl.BlockSpec((None, 8, 256), lambda s, k, j, pt: (s, 0, 2 * k + 1)),
           pl.BlockSpec((None, None, nst, 256), lambda s, k, j, pt: (layer, s, 0, k)),
           pl.BlockSpec((None, 8, 256), lambda s, k, j, pt: (s, 0, k)),
           per_k(blast), per_k(bnew), per_k(cfar), per_k(bwin), per_sk(32, 384), per_sk(32, 128)],
        out_specs=per_sk(32, 128),
        scratch_shapes=[pltpu.VMEM((1, 32, 1), F32), pltpu.VMEM((1, 32, 1), F32), pltpu.VMEM((1, 32, 128), F32)],
    )
    return pl.pallas_call(
        functools.partial(_nsa_s2_kernel, pps=pps, n_steps=n_steps),
        grid_spec=grid_spec,
        out_shape=jax.ShapeDtypeStruct((n_seq, C_KVH, 32, 128), F32),
        compiler_params=_cp(("parallel", "parallel", "arbitrary")),
    )(pt, qs, *([cache] * pps), addmask, addmask, c_new8, win_state, w_new8, blast, bnew, cfar, bwin, gates, ocmp)


def _final_kernel(x_ref, oa_ref, ob_ref, oc_ref, ga_ref, gb_ref, gc_ref, mg_ref, sub_ref, wb_ref, wo_ref, y_ref,
                  *, c_sub, d):
    y = None
    for n, (o_ref, g_ref) in enumerate(((oa_ref, ga_ref), (ob_ref, gb_ref), (oc_ref, gc_ref))):
        parts = []
        for h in range(8):
            sl = slice(h * LANE, (h + 1) * LANE)
            o = o_ref[:, sl]
            if n == 0:
                o = _rms(o, sub_ref[...]) * c_sub
            g = g_ref[:, sl]
            parts.append((o * (g * _sigmoid(g))).astype(BF16))
        proj = jnp.dot(jnp.concatenate(parts, axis=1), wb_ref[n], preferred_element_type=F32)
        term = _sigmoid(mg_ref[:, n * d:(n + 1) * d]) * proj
        y = term if y is None else y + term
    y_ref[...] = x_ref[...] + jnp.dot(y.astype(BF16), wo_ref[...], preferred_element_type=F32)


def _final(x, oa, ob, oc, z, sub, wb, wo, c_sub):
    m, d = x.shape
    tm = min(m, 128)
    row = lambda w, blk=0: pl.BlockSpec((tm, w), lambda i, blk=blk: (i, blk))
    once = lambda a: pl.BlockSpec(a.shape, lambda i: (0,) * a.ndim, pipeline_mode=pl.Buffered(1))
    return pl.pallas_call(
        functools.partial(_final_kernel, c_sub=c_sub, d=d),
        grid=(m // tm,),
        in_specs=[row(d), row(1024), row(1024), row(1024), row(1024, BLK_AG), row(1024, BLK_BG), row(1024, BLK_CG),
                  row(3 * d, 0), once(sub), once(wb), once(wo)],
        out_specs=row(d),
        out_shape=jax.ShapeDtypeStruct((m, d), F32),
        compiler_params=_cp(("parallel",), 56),
    )(x, oa, ob, oc, z, z, z, z, sub, wb, wo)


def _bucket_np(dist):
    n = np.maximum(dist, 0)
    exact = REL_BUCKETS // 2
    nf = np.maximum(n, exact).astype(np.float32)
    big = exact + (np.log(nf / np.float32(exact)) / np.float32(math.log(REL_MAX_DIST / exact))
                   * np.float32(REL_BUCKETS - exact)).astype(np.int32)
    return np.where(n < exact, n, np.minimum(big, REL_BUCKETS - 1)).astype(np.int32)


def _rel_table(rel, dist, valid):
    tab = jnp.take(rel.astype(F32), jnp.asarray(_bucket_np(dist)), axis=0)
    tab = jnp.where(jnp.asarray(valid)[..., None], tab, NEG)
    return jnp.moveaxis(tab, -1, 0)


def _prompt_tables(rel, tq, with_window):
    i = np.arange(tq)[:, None]
    j = np.arange(tq)[None, :]
    tabs = [_rel_table(rel, i - j, i >= j), _rel_table(rel, tq + i - j, np.ones((tq, tq), bool))]
    if with_window:
        dist = WINDOW + i - j
        tabs.append(_rel_table(rel, dist, dist <= WINDOW))
    return jnp.stack(tabs, axis=1)


def _layer_weights(l, rms_gain, w_in, b_forget, a_q_norm, a_k_norm, a_lambda, a_subln, b_q_norm, b_k_norm, c_q_norm,
                   c_k_norm, c_cmp_pe, c_cmp_w1, c_cmp_w2, w_branch, w_out):
    w = w_in[l]
    d = w.shape[0]
    w_perm = jnp.concatenate([w[:, a:b] for a, b in _PERM] + [jnp.zeros((d, N_PAD - N_USED + 96), F32)],
                             axis=1).astype(BF16)
    two = lambda g: jnp.concatenate([g, g])
    pn = jnp.stack([two(a_q_norm[l]), two(a_k_norm[l]), b_q_norm[l], b_k_norm[l], c_q_norm[l],
                    c_k_norm[l, 0], c_k_norm[l, 1], c_k_norm[l, 2]]).astype(F32)
    bf = jnp.zeros((1, LANE), F32).at[0, :B_HEADS].set(b_forget[l])
    al = a_lambda[l].astype(F32)
    lam_init = 0.8 - 0.6 * math.exp(-0.3 * l)
    lam = (jnp.exp(jnp.sum(al[0] * al[1])) - jnp.exp(jnp.sum(al[2] * al[3])) + lam_init).reshape(1)
    return dict(g=rms_gain[l], w=w_perm, pn=pn, bf=bf, lam=lam, c_sub=1.0 - lam_init,
                sub=a_subln[l].reshape(1, LANE), pe=c_cmp_pe[l], w1=c_cmp_w1[l].astype(BF16),
                w2=c_cmp_w2[l].astype(BF16), kn0=c_k_norm[l, 0].reshape(1, LANE),
                wb=w_branch[l].astype(BF16), wo=w_out[l].astype(BF16))


def _prompt_layer(x, lw, tb, tq):
    bsz, t, d = x.shape
    m = bsz * t
    z = _proj(x.reshape(m, d), lw["g"], lw["w"])
    qa, qb, qc, a_new, b_new, c_new, w_new, sm = _post(z, lw["pn"], lw["bf"])
    r3 = lambda a: a.reshape(bsz, t, a.shape[-1])
    qa, qb, qc, a_new, b_new, c_new, w_new, sm = map(r3, (qa, qb, qc, a_new, b_new, c_new, w_new, sm))
    logf = sm[..., :B_HEADS]
    o_a = _attn_a(qa, a_new, tb["a_tabs"], tb["a_far"], lw["lam"], tq)
    csum = jnp.cumsum(logf, axis=1).reshape(bsz, t, B_KVH, 2)
    o_b = _attn_b(qb, b_new, jnp.transpose(csum, (0, 2, 1, 3)), jnp.transpose(csum, (0, 2, 3, 1)), tb["tri"], tq)
    kc, vc = _cmp_p(c_new, lw["pe"], lw["w1"], lw["w2"], lw["kn0"], tb["ncp"])
    gates = jnp.transpose(sm[..., 8:8 + 3 * C_HEADS].reshape(bsz, t, C_KVH, 3 * C_GRP), (0, 2, 1, 3))
    o_c = _nsa_p(qc, kc, vc, c_new, w_new, gates, tb["c_cmp"], tb["c_tabs"], tb["expand"], tb["c_far"], tq)
    y = _final(x.reshape(m, d), o_a.reshape(m, 1024), o_b.reshape(m, 1024), o_c.reshape(m, 1024), z,
               lw["sub"], lw["wb"], lw["wo"], lw["c_sub"])
    keep = min(WINDOW, t)
    return (y.reshape(bsz, t, d), a_new.reshape(bsz, t, A_KVH, 256), b_new.reshape(bsz, t, B_KVH, 256), logf,
            c_new.reshape(bsz, t, C_KVH, 4, C_HD), w_new[:, t - keep:].reshape(bsz, keep, C_KVH, 2, C_HD))


def _stack_q(q, n_seq, t, kvh, grp):
    q = q.reshape(n_seq, t, kvh, grp, LANE)
    q = jnp.pad(q, ((0, 0), (0, 8 - t), (0, 0), (0, 0), (0, 0)))
    return jnp.transpose(q, (0, 2, 3, 1, 4)).reshape(n_seq, kvh, grp * 8, LANE)


def _unstack_o(o, n_seq, t, kvh, grp):
    o = o.reshape(n_seq, kvh, grp, 8, LANE)[:, :, :, :t]
    return jnp.transpose(o, (0, 3, 1, 2, 4)).reshape(n_seq * t, kvh * grp * LANE)


def _sample_layer(x, l, lw, tb, caches, page_table, pps):
    n_seq, t, d = x.shape
    m = n_seq * t
    cache_a, cache_b, cache_lf, cache_c, win_state = caches
    n_pages = page_table.shape[1]
    past = n_pages * 128
    z = _proj(x.reshape(m, d), lw["g"], lw["w"])
    qa, qb, qc, a_new, b_new, c_new, w_new, sm = _post(z, lw["pn"], lw["bf"])
    logf = sm[:, :B_HEADS].reshape(n_seq, t, B_HEADS)
    pad8 = lambda a: jnp.pad(a.reshape(n_seq, t, a.shape[-1]), ((0, 0), (0, 8 - t), (0, 0)))
    a_new8, b_new8, c_new8, w_new8 = map(pad8, (a_new, b_new, c_new, w_new))
    qa5 = qa.reshape(n_seq, t, A_KVH, 2, 2, A_HD)
    zq = jnp.zeros_like(qa5[..., 0, :])
    qa_m = jnp.stack([jnp.concatenate([qa5[..., 0, :], zq], -1), jnp.concatenate([zq, qa5[..., 1, :]], -1)], axis=4)
    qs_a = _stack_q(qa_m.reshape(m, A_KVH * 4 * LANE), n_seq, t, A_KVH, 4)
    o_a = _attn_a_s(page_table, lw["lam"], qs_a, cache_a, l, a_new8, tb["a_last"], tb["a_new"], tb["a_far"], pps)
    o_a = _unstack_o(o_a, n_seq, t, A_KVH, 2)
    lf_past = cache_lf[l][page_table].reshape(n_seq, past, B_HEADS).astype(F32)
    csum = jnp.cumsum(jnp.concatenate([lf_past, logf], axis=1), axis=1)
    cst = jnp.pad(jnp.transpose(csum, (0, 2, 1)), ((0, 0), (0, 0), (0, 128 - t))).reshape(n_seq, B_KVH, 2, past + 128)
    cq = jnp.pad(csum[:, past:], ((0, 0), (0, 8 - t), (0, 0))).reshape(n_seq, 8, B_KVH, 2)
    cqb = jnp.broadcast_to(jnp.transpose(cq, (0, 2, 3, 1)).reshape(n_seq, B_KVH, 16, 1), (n_seq, B_KVH, 16, LANE))
    o_b = _attn_b_s(page_table, _stack_q(qb, n_seq, t, B_KVH, 2), cqb, cache_b, l, cst, b_new8, tb["tri"], pps)
    o_b = _unstack_o(o_b, n_seq, t, B_KVH, 2)
    qs_c = _stack_q(qc, n_seq, t, C_KVH, C_GRP)
    n_sel = -(-(past + t) // L_BLK)
    ocmp, sel = _nsa_s1(page_table, qs_c, cache_c, l, lw["pe"], lw["w1"], lw["w2"], lw["kn0"], tb["c_cmp"], pps,
                        n_sel, tb["nsp"], past)
    selx = jnp.repeat(sel[..., :n_sel], L_BLK, axis=-1)
    addmask = (jnp.pad(selx, ((0, 0),) * 3 + ((0, past + 128 - n_sel * L_BLK),)) - 1.0) * (-NEG)
    g = jnp.pad(sm[:, 8:8 + 3 * C_HEADS].reshape(n_seq, t, C_KVH, C_GRP, 3), ((0, 0), (0, 8 - t)) + ((0, 0),) * 3)
    g = jnp.transpose(g, (0, 2, 4, 3, 1)).reshape(n_seq, C_KVH, 3, 32, 1)
    gates = jnp.transpose(jnp.broadcast_to(g, (n_seq, C_KVH, 3, 32, LANE)), (0, 1, 3, 2, 4)).reshape(n_seq, C_KVH, 32, 384)
    o_c = _nsa_s2(page_table, qs_c, cache_c, l, addmask, c_new8, win_state, w_new8, tb["c_last"], tb["c_new"],
                  tb["c_far"], tb["c_win"], gates, ocmp, pps)
    o_c = _unstack_o(o_c, n_seq, t, C_KVH, C_GRP)
    y = _final(x.reshape(m, d), o_a, o_b, o_c, z, lw["sub"], lw["wb"], lw["wo"], lw["c_sub"])
    w_all = jnp.concatenate([win_state[l].reshape(n_seq, -1, C_KVH, 2, C_HD), w_new.reshape(n_seq, t, C_KVH, 2, C_HD)], 1)
    keep = win_state.shape[2]
    return (y.reshape(n_seq, t, d), a_new.reshape(n_seq, t, A_KVH, 256), b_new.reshape(n_seq, t, B_KVH, 256), logf,
            c_new.reshape(n_seq, t, C_KVH, 4, C_HD), w_all[:, w_all.shape[1] - keep:])


def _sample_tables(rel_a, rel_c, t, past, n_cmp, nst):
    tok = np.arange(8)
    jj = np.arange(LANE)
    d_last = LANE + tok[:, None] - jj[None, :]
    d_new = tok[:, None] - jj[None, :]
    v_new = (d_new >= 0) & (jj[None, :] < t)
    ones = np.ones((8, LANE), bool)

    def rows(rel, dist, valid, kvh, per):
        tab = _rel_table(rel, dist, valid)
        return tab.reshape(kvh, per * 8, dist.shape[1])

    tb = dict(a_last=rows(rel_a, d_last, ones, A_KVH, 4), a_new=rows(rel_a, d_new, v_new, A_KVH, 4),
              c_last=rows(rel_c, d_last, ones, C_KVH, C_GRP), c_new=rows(rel_c, d_new, v_new, C_KVH, C_GRP))
    far = lambda rel, kvh, per: jnp.broadcast_to(
        jnp.repeat(rel[REL_BUCKETS - 1].astype(F32), 8).reshape(kvh, per * 8, 1), (kvh, per * 8, LANE))
    tb["a_far"] = far(rel_a, A_KVH, 4)
    tb["c_far"] = far(rel_c, C_KVH, C_GRP)
    ws = np.arange(nst)
    d_win = nst + tok[:, None] - ws[None, :]
    tb["c_win"] = rows(rel_c, d_win, (d_win >= 0) & (d_win <= WINDOW), C_KVH, C_GRP)
    d_cmp = past + tok[:, None] - (L_BLK * np.arange(n_cmp)[None, :] + L_BLK - 1)
    tb["c_cmp"] = rows(rel_c, d_cmp, d_cmp >= 0, C_KVH, C_GRP)
    tri = np.where(v_new, 0.0, NEG).astype(np.float32)
    tb["tri"] = jnp.asarray(np.concatenate([tri, tri], axis=0))
    return tb


def kernel(x_prompt, x_sample, cache_a_kv, cache_b_kv, cache_b_logf, cache_c_kv, state_c_win, page_table, rms_gain,
           w_in, b_forget, a_q_norm, a_k_norm, a_lambda, a_subln, b_q_norm, b_k_norm, c_q_norm, c_k_norm, c_cmp_pe,
           c_cmp_w1, c_cmp_w2, rel_bias, w_branch, w_out):
    depth = w_in.shape[0]
    bsz, t, d = x_prompt.shape
    n_seq, t_s, _ = x_sample.shape
    n_phys = cache_a_kv.shape[1]
    n_pages = page_table.shape[1]
    past = n_pages * cache_a_kv.shape[2]
    nst = state_c_win.shape[2]
    assert cache_a_kv.shape[2] == 128 and past % L_BLK == 0 and t_s <= 8 and nst == WINDOW and past >= WINDOW
    tq = min(256, t)
    assert t % tq == 0 and WINDOW % tq == 0 and WINDOW // tq >= 2 and tq % L_BLK == 0
    pps = 8
    assert n_pages % pps == 0

    rel_a = rel_bias[:, :2 * A_HEADS]
    rel_c = rel_bias[:, 2 * A_HEADS:]
    n_cmp_p = t // L_BLK
    ncp = -(-n_cmp_p // LANE) * LANE
    nsp_p = -(-(-(-t // L_BLK)) // LANE) * LANE
    pos = np.arange(t)[:, None]
    blk = np.arange(ncp)[None, :]
    d_cmp = pos - (L_BLK * blk + L_BLK - 1)
    tri = np.where(np.arange(tq)[:, None] >= np.arange(tq)[None, :], 0.0, NEG).astype(np.float32)
    expand = (np.arange(t)[None, :] // L_BLK == np.arange(nsp_p)[:, None]).astype(np.float32)
    tb_p = dict(
        a_tabs=_prompt_tables(rel_a, tq, False).reshape(A_KVH, 4, 2, tq, tq).transpose(0, 2, 1, 3, 4),
        a_far=rel_a[REL_BUCKETS - 1].astype(F32),
        c_tabs=_prompt_tables(rel_c, tq, True).reshape(C_KVH, C_GRP, 3, tq, tq).transpose(0, 2, 1, 3, 4),
        c_far=rel_c[REL_BUCKETS - 1].astype(F32),
        c_cmp=_rel_table(rel_c, d_cmp, (d_cmp >= 0) & (blk < n_cmp_p)),
        tri=jnp.asarray(tri), expand=jnp.asarray(expand, dtype=BF16), ncp=ncp)
    n_sel_s = -(-(past + t_s) // L_BLK)
    tb_s = _sample_tables(rel_a, rel_c, t_s, past, past // L_BLK, nst)
    tb_s["nsp"] = -(-n_sel_s // LANE) * LANE

    caches = (cache_a_kv.reshape(depth, n_phys, 128, 1024), cache_b_kv.reshape(depth, n_phys, 128, 1024),
              cache_b_logf, cache_c_kv.reshape(depth, n_phys, 128, 1024),
              state_c_win.reshape(depth, n_seq, nst, C_KVH * 2 * C_HD))
    yp, ys = x_prompt, x_sample
    st_p, st_s = [], []
    for l in range(depth):
        lw = _layer_weights(l, rms_gain, w_in, b_forget, a_q_norm, a_k_norm, a_lambda, a_subln, b_q_norm, b_k_norm,
                            c_q_norm, c_k_norm, c_cmp_pe, c_cmp_w1, c_cmp_w2, w_branch, w_out)
        yp, *sp = _prompt_layer(yp, lw, tb_p, tq)
        ys, *ss = _sample_layer(ys, l, lw, tb_s, caches, page_table, pps)
        st_p.append(sp)
        st_s.append(ss)
    stk = lambda lst, i: jnp.stack([s[i] for s in lst])
    return (yp, ys, stk(st_p, 0), stk(st_s, 0), stk(st_p, 1), stk(st_s, 1), stk(st_p, 2), stk(st_s, 2),
            stk(st_p, 3), stk(st_s, 3), stk(st_p, 4), stk(st_s, 4))
```

```python
import functools
import math

import numpy as np
import jax
import jax.numpy as jnp
from jax import lax
from jax.experimental import pallas as pl
from jax.experimental.pallas import tpu as pltpu

F32 = jnp.float32
BF16 = jnp.bfloat16

A_HD, A_VD, A_HEADS, A_KVH = 64, 128, 8, 4
B_HD, B_HEADS, B_KVH = 128, 8, 4
C_HD, C_HEADS, C_KVH, C_GRP = 128, 8, 2, 4
L_BLK, N_SEL, WINDOW = 64, 16, 512
REL_BUCKETS, REL_MAX_DIST = 32, 128
EPS, NEG, FORCE = 1e-6, -1e30, 1e6
SCALE_A, SCALE_BC = A_HD ** -0.5, 128 ** -0.5
LANE = 128
PAGE = 128
VMEM_MB = 48

_PERM = [(9760, 15904), (0, 1024), (2048, 3072), (3072, 4096), (5128, 6152), (6152, 7176), (8736, 9760),
         (1024, 1536), (1536, 2048), (4096, 4608), (4608, 5120), (7176, 8712), (5120, 5128), (8712, 8736)]
N_USED = 16000
N_PAD = 16128
TN = 768
BLK_AQ, BLK_AG, BLK_BQ, BLK_BG, BLK_CQ, BLK_CG = 6, 7, 8, 9, 10, 11
BLK_AK, BLK_AV, BLK_BK, BLK_BV, BLK_C0, BLK_C1, BLK_C2 = 24, 25, 26, 27, 28, 29, 30
BLK_SM = 124


def _cp(sem, mb=VMEM_MB):
    return pltpu.CompilerParams(dimension_semantics=sem, vmem_limit_bytes=mb * 1024 * 1024)


def _smem():
    return pl.BlockSpec(memory_space=pltpu.SMEM)


def _rms(x, g):
    ms = jnp.mean(x * x, axis=-1, keepdims=True)
    return x * lax.rsqrt(ms + EPS) * g


def _rms_half(x, g2):
    sq = x * x
    lo = lax.broadcasted_iota(jnp.int32, x.shape, 1) < 64
    s_lo = jnp.sum(jnp.where(lo, sq, 0.0), axis=-1, keepdims=True)
    s_hi = jnp.sum(jnp.where(lo, 0.0, sq), axis=-1, keepdims=True)
    ms = jnp.where(lo, s_lo, s_hi) * (1.0 / 64)
    return x * lax.rsqrt(ms + EPS) * g2


def _sigmoid(x):
    return 1.0 / (1.0 + jnp.exp(-x))


def _dot_t(a, b):
    return lax.dot_general(a, b, (((1,), (1,)), ((), ())), preferred_element_type=F32)


def _lanes(x, n):
    return x if n == LANE else jnp.concatenate([x] * (n // LANE), axis=1)


def _rows(x, times):
    return x if times == 1 else jnp.concatenate([x] * times, axis=0)


def _fa_update(s, vgroups, m_ref, l_ref, acc_ref, shift=None):
    m_prev = m_ref[...]
    rmax = jnp.max(s, axis=-1, keepdims=True)
    if shift is not None:
        rmax = rmax + shift
    m_new = jnp.maximum(m_prev, rmax)
    alpha = jnp.exp(m_prev - m_new)
    off = m_new if shift is None else m_new - shift
    p = jnp.exp(s - _lanes(off, s.shape[1]))
    l_ref[...] = alpha * l_ref[...] + jnp.sum(p, axis=-1, keepdims=True)
    pb = p.astype(BF16)
    rg = s.shape[0] // len(vgroups)
    pv = []
    for gi, vs in enumerate(vgroups):
        lo, part = 0, None
        for v in vs:
            d = jnp.dot(pb[gi * rg:(gi + 1) * rg, lo:lo + v.shape[0]], v, preferred_element_type=F32)
            part = d if part is None else part + d
            lo += v.shape[0]
        pv.append(part)
    acc_ref[...] = alpha * acc_ref[...] + (pv[0] if len(pv) == 1 else jnp.concatenate(pv, axis=0))
    m_ref[...] = m_new


def _fa_init(m_ref, l_ref, acc_ref):
    m_ref[...] = jnp.full(m_ref.shape, NEG, F32)
    l_ref[...] = jnp.zeros(l_ref.shape, F32)
    acc_ref[...] = jnp.zeros(acc_ref.shape, F32)


def _fa_scratch(rows):
    return [pltpu.VMEM((rows, LANE), F32)] * 3


def _proj_kernel(x_ref, g_ref, w_ref, o_ref, h_ref):
    @pl.when(pl.program_id(1) == 0)
    def _():
        h_ref[...] = _rms(x_ref[...], g_ref[...]).astype(BF16)

    o_ref[...] = jnp.dot(h_ref[...], w_ref[...], preferred_element_type=F32)


def _proj(x, g, w):
    m, d = x.shape
    tm = next(c for c in (1024, 512, 256, 128, m) if m % c == 0)
    return pl.pallas_call(
        _proj_kernel,
        grid=(m // tm, N_PAD // TN),
        in_specs=[pl.BlockSpec((tm, d), lambda i, j: (i, 0)),
                  pl.BlockSpec((1, d), lambda i, j: (0, 0)),
                  pl.BlockSpec((d, TN), lambda i, j: (0, j))],
        out_specs=pl.BlockSpec((tm, TN), lambda i, j: (i, j)),
        out_shape=jax.ShapeDtypeStruct((m, N_PAD), F32),
        scratch_shapes=[pltpu.VMEM((tm, d), BF16)],
        compiler_params=_cp(("parallel", "arbitrary")),
        name="proj",
    )(x, g.reshape(1, d), w)


def _post_kernel(zaq, zbq, zcq, zak, zav, zbk, zbv, zc0, zc1, zc2, zsm, pn, bf,
                 qa, qb, qc, anew, bnew, cnew, wnew, sm):
    for h in range(8):
        sl = slice(h * LANE, (h + 1) * LANE)
        qa[:, sl] = (_rms_half(zaq[:, sl], pn[0:1]) * SCALE_A).astype(BF16)
        qb[:, sl] = (_rms(zbq[:, sl], pn[2:3]) * SCALE_BC).astype(BF16)
        qc[:, sl] = (_rms(zcq[:, sl], pn[4:5]) * SCALE_BC).astype(BF16)
    for k in range(4):
        src = slice(k * LANE, (k + 1) * LANE)
        anew[:, k * 256:k * 256 + 128] = _rms_half(zak[:, src], pn[1:2])
        anew[:, k * 256 + 128:(k + 1) * 256] = zav[:, src]
        bnew[:, k * 256:k * 256 + 128] = _rms(zbk[:, src], pn[3:4])
        bnew[:, k * 256 + 128:(k + 1) * 256] = zbv[:, src]
    for k in range(2):
        s0 = slice(k * LANE, (k + 1) * LANE)
        s1 = slice(256 + k * LANE, 256 + (k + 1) * LANE)
        cnew[:, k * 512:k * 512 + 128] = zc0[:, s0]
        cnew[:, k * 512 + 128:k * 512 + 256] = zc0[:, s1]
        cnew[:, k * 512 + 256:k * 512 + 384] = _rms(zc1[:, s0], pn[6:7])
        cnew[:, k * 512 + 384:k * 512 + 512] = zc1[:, s1]
        wnew[:, k * 256:k * 256 + 128] = _rms(zc2[:, s0], pn[7:8])
        wnew[:, k * 256 + 128:k * 256 + 256] = zc2[:, s1]
    z = zsm[...]
    x = z + bf[...]
    logf = jnp.minimum(x, 0.0) - jnp.log1p(jnp.exp(-jnp.abs(x)))
    lane = lax.broadcasted_iota(jnp.int32, z.shape, 1)
    sm[...] = jnp.where(lane < 8, logf, _sigmoid(z))


def _post(z, pn, bf):
    m = z.shape[0]
    tm = min(m, 256)

    def zs(width, blk):
        return pl.BlockSpec((tm, width), lambda i, blk=blk: (i, blk))

    def os(width):
        return pl.BlockSpec((tm, width), lambda i: (i, 0))

    const = lambda a: pl.BlockSpec(a.shape, lambda i: (0, 0))
    return pl.pallas_call(
        _post_kernel,
        grid=(m // tm,),
        in_specs=[zs(1024, BLK_AQ), zs(1024, BLK_BQ), zs(1024, BLK_CQ), zs(512, BLK_AK), zs(512, BLK_AV),
                  zs(512, BLK_BK), zs(512, BLK_BV), zs(512, BLK_C0), zs(512, BLK_C1), zs(512, BLK_C2),
                  zs(128, BLK_SM), const(pn), const(bf)],
        out_specs=[os(1024), os(1024), os(1024), os(1024), os(1024), os(1024), os(512), os(128)],
        out_shape=[jax.ShapeDtypeStruct((m, 1024), BF16)] * 3
        + [jax.ShapeDtypeStruct((m, 1024), F32)] * 3
        + [jax.ShapeDtypeStruct((m, 512), F32), jax.ShapeDtypeStruct((m, 128), F32)],
        compiler_params=_cp(("parallel",)),
        name="post",
    )(*([z] * 11), pn, bf)


def _attn_a_kernel(cfar_ref, lam_ref, q_ref, kv_ref, tab_ref, o_ref, m_ref, l_ref, acc_ref, cvec_ref, *, tq, tk):
    kh = pl.program_id(1)
    i = pl.program_id(2)
    q = q_ref[...]
    lo = lax.broadcasted_iota(jnp.int32, (tq, LANE), 1) < 64
    zero = jnp.zeros((tq, LANE), BF16)
    blocks = []
    for g in range(2):
        qh = q[:, g * LANE:(g + 1) * LANE]
        blocks.append(jnp.where(lo, qh, zero))
        blocks.append(jnp.where(lo, zero, qh))
    qs = jnp.concatenate(blocks, axis=0)
    for b in range(4):
        cvec_ref[b * tq:(b + 1) * tq, :] = jnp.full((tq, LANE), cfar_ref[kh * 4 + b], F32)
    _fa_init(m_ref, l_ref, acc_ref)

    def tile(kt, bias, shift):
        r0 = pl.multiple_of(kt * tk, tk)
        k = kv_ref[pl.ds(r0, tk), 0:128].astype(BF16)
        v = kv_ref[pl.ds(r0, tk), 128:256].astype(BF16)
        s = _dot_t(qs, k)
        if bias is not None:
            s = s + bias
        _fa_update(s, [[v]], m_ref, l_ref, acc_ref, shift)

    def far(kt, c):
        tile(kt, None, cvec_ref[...])
        return c

    lax.fori_loop(0, jnp.maximum(i - 1, 0), far, 0)

    @pl.when(i >= 1)
    def _():
        tile(i - 1, tab_ref[1], None)

    tile(i, tab_ref[0], None)
    lam = lam_ref[0]
    o = acc_ref[...] / l_ref[...]
    for g in range(2):
        o_ref[:, g * LANE:(g + 1) * LANE] = o[2 * g * tq:(2 * g + 1) * tq] - lam * o[(2 * g + 1) * tq:(2 * g + 2) * tq]


def _attn_a(qa, a_new, tabs, cfar, lam, tq):
    bsz, t, _ = qa.shape
    return pl.pallas_call(
        functools.partial(_attn_a_kernel, tq=tq, tk=tq),
        grid=(bsz, A_KVH, t // tq),
        in_specs=[_smem(), _smem(),
                  pl.BlockSpec((None, tq, 256), lambda b, k, i: (b, i, k)),
                  pl.BlockSpec((None, t, 256), lambda b, k, i: (b, 0, k)),
                  pl.BlockSpec((None, 2, 4 * tq, tq), lambda b, k, i: (k, 0, 0, 0))],
        out_specs=pl.BlockSpec((None, tq, 256), lambda b, k, i: (b, i, k)),
        out_shape=jax.ShapeDtypeStruct((bsz, t, 1024), F32),
        scratch_shapes=_fa_scratch(4 * tq) + [pltpu.VMEM((4 * tq, LANE), F32)],
        compiler_params=_cp(("parallel", "parallel", "arbitrary")),
        name="attn_a",
    )(cfar, lam, qa, a_new, tabs)


def _attn_b_kernel(q_ref, kv_ref, cq_ref, ck_ref, tri_ref, o_ref, m_ref, l_ref, acc_ref, cqb_ref, *, tq, tk):
    i = pl.program_id(2)
    q = q_ref[...]
    qs = jnp.concatenate([q[:, 0:128], q[:, 128:256]], axis=0)
    for g in range(2):
        cqb_ref[g * tq:(g + 1) * tq, :] = jnp.broadcast_to(cq_ref[:, g:g + 1], (tq, LANE))
    _fa_init(m_ref, l_ref, acc_ref)

    def tile(kt, diag):
        r0 = pl.multiple_of(kt * tk, tk)
        k = kv_ref[pl.ds(r0, tk), 0:128].astype(BF16)
        v = kv_ref[pl.ds(r0, tk), 128:256].astype(BF16)
        decay = jnp.concatenate(
            [_lanes(cqb_ref[g * tq:(g + 1) * tq, :], tk) - ck_ref[g:g + 1, pl.ds(r0, tk)] for g in range(2)], axis=0)
        s = _dot_t(qs, k) + decay
        if diag:
            s = s + tri_ref[...]
        _fa_update(s, [[v]], m_ref, l_ref, acc_ref)

    def far(kt, c):
        tile(kt, False)
        return c

    lax.fori_loop(0, i, far, 0)
    tile(i, True)
    o = acc_ref[...] / l_ref[...]
    for g in range(2):
        o_ref[:, g * LANE:(g + 1) * LANE] = o[g * tq:(g + 1) * tq]


def _attn_b(qb, b_new, csq, cst, tri, tq):
    bsz, t, _ = qb.shape
    return pl.pallas_call(
        functools.partial(_attn_b_kernel, tq=tq, tk=tq),
        grid=(bsz, B_KVH, t // tq),
        in_specs=[pl.BlockSpec((None, tq, 256), lambda b, k, i: (b, i, k)),
                  pl.BlockSpec((None, t, 256), lambda b, k, i: (b, 0, k)),
                  pl.BlockSpec((None, None, tq, 2), lambda b, k, i: (b, k, i, 0)),
                  pl.BlockSpec((None, None, 2, t), lambda b, k, i: (b, k, 0, 0)),
                  pl.BlockSpec((2 * tq, tq), lambda b, k, i: (0, 0))],
        out_specs=pl.BlockSpec((None, tq, 256), lambda b, k, i: (b, i, k)),
        out_shape=jax.ShapeDtypeStruct((bsz, t, 1024), F32),
        scratch_shapes=_fa_scratch(2 * tq) + [pltpu.VMEM((2 * tq, LANE), F32)],
        compiler_params=_cp(("parallel", "parallel", "arbitrary")),
        name="attn_b",
    )(qb, b_new, csq, cst, tri)


def _compress(x_refs, flat_ref, pe_ref, w1_ref, w2_ref, kn_ref, n_cmp):
    outs = []
    for c in range(2):
        for l in range(L_BLK):
            rows = x_refs[c][pl.ds(l, n_cmp, stride=L_BLK), :]
            flat_ref[:, l * LANE:(l + 1) * LANE] = (rows + pe_ref[c, l:l + 1, :]).astype(BF16)
        h = jnp.dot(flat_ref[...], w1_ref[c], preferred_element_type=F32)
        h = h * _sigmoid(h)
        o = jnp.dot(h.astype(BF16), w2_ref[c], preferred_element_type=F32)
        if c == 0:
            o = _rms(o, kn_ref[...])
        outs.append(o)
    return outs


def _cmp_and_select(qg, kc, vc, bias_of, q0, rows, nsp, n_sel):
    ncp = kc.shape[0]
    imp = jnp.zeros((rows, ncp), F32)
    ocmp = []
    for g, q in enumerate(qg):
        bias = bias_of(g)
        s = _dot_t(q, kc) + bias
        e = jnp.exp(s - jnp.max(s, axis=-1, keepdims=True))
        pc = jnp.where(bias > 0.5 * NEG, e / jnp.sum(e, axis=-1, keepdims=True), 0.0)
        imp = imp + pc
        ocmp.append(jnp.dot(pc.astype(BF16), vc, preferred_element_type=F32))
    if nsp > ncp:
        imp = jnp.concatenate([imp, jnp.zeros((rows, nsp - ncp), F32)], axis=1)
    j = lax.broadcasted_iota(jnp.int32, (rows, nsp), 1)
    pos = q0 + lax.broadcasted_iota(jnp.int32, (rows, nsp), 0)
    cur = lax.shift_right_logical(pos, 6)
    imp = imp + FORCE * (jnp.where(j == cur, 4.0, 0.0) + jnp.where(j == cur - 1, 2.0, 0.0)
                         + jnp.where(j == 0, 1.0, 0.0))
    imp = jnp.where(j <= cur, imp, NEG)
    cnt = jnp.zeros((rows, nsp), F32)
    for c in range(n_sel):
        col = imp[:, c:c + 1]
        first = jnp.where(j > c, 1.0, 0.0)
        cnt = cnt + jnp.where(col > imp, 1.0, jnp.where(col == imp, first, 0.0))
    sel = jnp.where(cnt < float(min(N_SEL, n_sel)), 1.0, 0.0)
    return ocmp, sel


def _cmp_p_kernel(xk_ref, xv_ref, pe_ref, w1_ref, w2_ref, kn_ref, kc_ref, vc_ref, flat_ref, *, n_cmp):
    kc, vc = _compress((xk_ref, xv_ref), flat_ref, pe_ref, w1_ref, w2_ref, kn_ref, n_cmp)
    kc_ref[...] = jnp.zeros(kc_ref.shape, BF16)
    vc_ref[...] = jnp.zeros(vc_ref.shape, BF16)
    kc_ref[0:n_cmp, :] = kc.astype(BF16)
    vc_ref[0:n_cmp, :] = vc.astype(BF16)


def _cmp_p(c_new, pe, w1, w2, kn0, ncp):
    bsz, t, _ = c_new.shape
    n_cmp = t // L_BLK
    whole = lambda a: pl.BlockSpec(a.shape, lambda b, k: (0,) * a.ndim)
    return pl.pallas_call(
        functools.partial(_cmp_p_kernel, n_cmp=n_cmp),
        grid=(bsz, C_KVH),
        in_specs=[pl.BlockSpec((None, t, 128), lambda b, k: (b, 0, 4 * k)),
                  pl.BlockSpec((None, t, 128), lambda b, k: (b, 0, 4 * k + 1)),
                  whole(pe), whole(w1), whole(w2), whole(kn0)],
        out_specs=[pl.BlockSpec((None, None, ncp, 128), lambda b, k: (b, k, 0, 0))] * 2,
        out_shape=[jax.ShapeDtypeStruct((bsz, C_KVH, ncp, 128), BF16)] * 2,
        scratch_shapes=[pltpu.VMEM((n_cmp, L_BLK * C_HD), BF16)],
        compiler_params=_cp(("parallel", "parallel")),
        name="nsa_cmp_p",
    )(c_new, c_new, pe, w1, w2, kn0)


def _nsa_p_kernel(cfar_ref, q_ref, kc_ref, vc_ref, sel_ref, win_ref, gt_ref, bc_ref, tab_ref, e_ref, o_ref,
                  ms_ref, ls_ref, accs_ref, mw_ref, lw_ref, accw_ref, cvec_ref, am_ref, *, tq, tk, n_sel, nsp, nw):
    kh = pl.program_id(1)
    i = pl.program_id(2)
    q = q_ref[...]
    qg = [q[:, g * LANE:(g + 1) * LANE] for g in range(C_GRP)]
    qs = jnp.concatenate(qg, axis=0)
    ocmp, sel = _cmp_and_select(qg, kc_ref[...], vc_ref[...], lambda g: bc_ref[g], i * tq, tq, nsp, n_sel)
    am_ref[...] = (jnp.dot(sel.astype(BF16), e_ref[...], preferred_element_type=F32) - 1.0) * (-NEG)
    for g in range(C_GRP):
        cvec_ref[g * tq:(g + 1) * tq, :] = jnp.full((tq, LANE), cfar_ref[kh * C_GRP + g], F32)
    _fa_init(ms_ref, ls_ref, accs_ref)
    _fa_init(mw_ref, lw_ref, accw_ref)

    def tile(src_ref, kt, bias, shift, masked, state):
        r0 = pl.multiple_of(kt * tk, tk)
        k = src_ref[pl.ds(r0, tk), 0:128].astype(BF16)
        v = src_ref[pl.ds(r0, tk), 128:256].astype(BF16)
        s = _dot_t(qs, k)
        if bias is not None:
            s = s + bias
        if masked:
            s = s + _rows(am_ref[:, pl.ds(r0, tk)], C_GRP)
        _fa_update(s, [[v]], *state, shift)

    sel_state = (ms_ref, ls_ref, accs_ref)
    win_state = (mw_ref, lw_ref, accw_ref)

    def far(kt, c):
        tile(sel_ref, kt, None, cvec_ref[...], True, sel_state)
        return c

    lax.fori_loop(0, jnp.maximum(i - 1, 0), far, 0)

    @pl.when(i >= 1)
    def _():
        tile(sel_ref, i - 1, tab_ref[1], None, True, sel_state)
        tile(win_ref, i - 1, tab_ref[1], None, False, win_state)

    tile(sel_ref, i, tab_ref[0], None, True, sel_state)
    tile(win_ref, i, tab_ref[0], None, False, win_state)
    for d in range(2, nw):
        @pl.when(i >= d)
        def _(d=d):
            tile(win_ref, i - d, None, cvec_ref[...], False, win_state)

    @pl.when(i >= nw)
    def _():
        tile(win_ref, i - nw, tab_ref[2], None, False, win_state)

    gt = gt_ref[...]
    o_sel = accs_ref[...] / ls_ref[...]
    o_win = accw_ref[...] / lw_ref[...]
    for g in range(C_GRP):
        rs = slice(g * tq, (g + 1) * tq)
        o_ref[:, g * LANE:(g + 1) * LANE] = (gt[:, 3 * g:3 * g + 1] * ocmp[g] + gt[:, 3 * g + 1:3 * g + 2] * o_sel[rs]
                                            + gt[:, 3 * g + 2:3 * g + 3] * o_win[rs])


def _nsa_p(qc, kc, vc, c_new, w_new, gates, bias_cmp, tabs, expand, cfar, tq):
    bsz, t, _ = qc.shape
    ncp = kc.shape[2]
    n_sel = -(-t // L_BLK)
    nsp = expand.shape[0]
    kern = functools.partial(_nsa_p_kernel, tq=tq, tk=tq, n_sel=n_sel, nsp=nsp, nw=WINDOW // tq)
    rows = C_GRP * tq
    return pl.pallas_call(
        kern,
        grid=(bsz, C_KVH, t // tq),
        in_specs=[_smem(),
                  pl.BlockSpec((None, tq, 512), lambda b, k, i: (b, i, k)),
                  pl.BlockSpec((None, None, ncp, 128), lambda b, k, i: (b, k, 0, 0)),
                  pl.BlockSpec((None, None, ncp, 128), lambda b, k, i: (b, k, 0, 0)),
                  pl.BlockSpec((None, t, 256), lambda b, k, i: (b, 0, 2 * k + 1)),
                  pl.BlockSpec((None, t, 256), lambda b, k, i: (b, 0, k)),
                  pl.BlockSpec((None, None, tq, 12), lambda b, k, i: (b, k, i, 0)),
                  pl.BlockSpec((C_GRP, tq, ncp), lambda b, k, i: (k, i, 0)),
                  pl.BlockSpec((None, 3, rows, tq), lambda b, k, i: (k, 0, 0, 0)),
                  pl.BlockSpec(expand.shape, lambda b, k, i: (0, 0))],
        out_specs=pl.BlockSpec((None, tq, 512), lambda b, k, i: (b, i, k)),
        out_shape=jax.ShapeDtypeStruct((bsz, t, 1024), F32),
        scratch_shapes=_fa_scratch(rows) + _fa_scratch(rows) + [pltpu.VMEM((rows, LANE), F32), pltpu.VMEM((tq, t), F32)],
        compiler_params=_cp(("parallel", "parallel", "arbitrary")),
        name="nsa_p",
    )(cfar, qc, kc, vc, c_new, w_new, gates, bias_cmp, tabs, expand)


SUBROWS = 8


def _pad_rows(x, n):
    return jnp.concatenate([x, jnp.zeros((n - x.shape[0], x.shape[1]), x.dtype)], axis=0)


def _sub(page_ref, s):
    return page_ref[pl.ds(s, PAGE, stride=SUBROWS), :]


def _head_rows(win_ref, kh):
    return win_ref.reshape(4 * PAGE, LANE)[pl.ds(kh, PAGE, stride=4), :]


def _attn_a_s_kernel(pt_ref, lam_ref, q_ref, *rest, pps, n_steps):
    kwin = [lambda kh, w=w: _head_rows(w, kh) for w in rest[:pps]]
    vwin = [lambda kh, w=w: _head_rows(w, kh) for w in rest[pps:2 * pps]]
    new_ref, blast_ref, bnew_ref, cfar_ref, o_ref, m_ref, l_ref, acc_ref = rest[2 * pps:]
    j = pl.program_id(1)
    last = j == n_steps - 1

    @pl.when(j == 0)
    def _():
        _fa_init(m_ref, l_ref, acc_ref)

    s, vgroups = [], []
    for kh in range(A_KVH):
        q = q_ref[kh]
        far = cfar_ref[kh]
        row = [_dot_t(q, w(kh).astype(BF16)) + far for w in kwin[:-1]]
        row.append(_dot_t(q, kwin[-1](kh).astype(BF16)) + jnp.where(last, blast_ref[kh], far))
        s.append(jnp.concatenate(row, axis=1))
        vgroups.append([w(kh).astype(BF16) for w in vwin])
    _fa_update(jnp.concatenate(s, axis=0), vgroups, m_ref, l_ref, acc_ref)

    @pl.when(last)
    def _():
        lam = lam_ref[0]
        s, vgroups = [], []
        for kh in range(A_KVH):
            kn = _pad_rows(new_ref[:, kh * 256:kh * 256 + 128], LANE).astype(BF16)
            s.append(_dot_t(q_ref[kh], kn) + bnew_ref[kh])
            vgroups.append([_pad_rows(new_ref[:, kh * 256 + 128:(kh + 1) * 256], LANE).astype(BF16)])
        _fa_update(jnp.concatenate(s, axis=0), vgroups, m_ref, l_ref, acc_ref)
        o = acc_ref[...] / l_ref[...]
        for kh in range(A_KVH):
            for g in range(2):
                r = kh * 32 + 16 * g
                o_ref[kh, g * 8:(g + 1) * 8, :] = o[r:r + 8] - lam * o[r + 8:r + 16]


def _kv_page_specs(layer, pps):
    def spec(u, half):
        return pl.BlockSpec((None, None, PAGE, 4, LANE),
                            lambda s, j, pt: (layer, pt[s, j * pps + u], 0, 0, half))
    return [spec(u, 0) for u in range(pps)] + [spec(u, 1) for u in range(pps)]


def _page_specs(layer, pps):
    return [pl.BlockSpec((None, None, PAGE * SUBROWS, LANE),
                         lambda s, j, pt, u=u: (layer, pt[s, j * pps + u], 0, 0)) for u in range(pps)]


def _attn_a_s(pt, lam, qs, cache, layer, new, blast, bnew, cfar, pps):
    n_seq, n_pages = pt.shape
    n_steps = n_pages // pps
    c3 = lambda a: pl.BlockSpec(a.shape, lambda s, j, pt: (0, 0, 0))
    grid_spec = pltpu.PrefetchScalarGridSpec(
        num_scalar_prefetch=1,
        grid=(n_seq, n_steps),
        in_specs=[_smem(), pl.BlockSpec((None, 4, 32, 128), lambda s, j, pt: (s, 0, 0, 0))]
        + _kv_page_specs(layer, pps)
        + [pl.BlockSpec((None, 8, 1024), lambda s, j, pt: (s, 0, 0)), c3(blast), c3(bnew), c3(cfar)],
        out_specs=pl.BlockSpec((None, 4, 16, 128), lambda s, j, pt: (s, 0, 0, 0)),
        scratch_shapes=_fa_scratch(A_KVH * 32),
    )
    return pl.pallas_call(
        functools.partial(_attn_a_s_kernel, pps=pps, n_steps=n_steps),
        grid_spec=grid_spec,
        out_shape=jax.ShapeDtypeStruct((n_seq, 4, 16, 128), F32),
        compiler_params=_cp(("parallel", "arbitrary")),
        name="attn_a_s",
    )(pt, lam, qs, *([cache] * (2 * pps)), new, blast, bnew, cfar)


def _attn_b_s_kernel(pt_ref, q_ref, cq_ref, *rest, pps, n_steps):
    kwin, vwin = rest[:pps], rest[pps:2 * pps]
    ck_ref, ckn_ref, new_ref, tri_ref, o_ref, m_ref, l_ref, acc_ref = rest[2 * pps:]
    j = pl.program_id(1)

    @pl.when(j == 0)
    def _():
        _fa_init(m_ref, l_ref, acc_ref)

    def decay(ck, kh, lo):
        rows = [jnp.broadcast_to(ck[kh, g:g + 1, lo:lo + LANE], (8, LANE)) for g in range(2)]
        return cq_ref[kh] - jnp.concatenate(rows, axis=0)

    s, vgroups = [], []
    for kh in range(B_KVH):
        s.append(jnp.concatenate([_dot_t(q_ref[kh], _head_rows(w, kh).astype(BF16)) + decay(ck_ref, kh, u * LANE)
                                  for u, w in enumerate(kwin)], axis=1))
        vgroups.append([_head_rows(w, kh).astype(BF16) for w in vwin])
    _fa_update(jnp.concatenate(s, axis=0), vgroups, m_ref, l_ref, acc_ref)

    @pl.when(j == n_steps - 1)
    def _():
        s, vgroups = [], []
        for kh in range(B_KVH):
            kn = _pad_rows(new_ref[:, kh * 256:kh * 256 + 128], LANE).astype(BF16)
            s.append(_dot_t(q_ref[kh], kn) + decay(ckn_ref, kh, 0) + tri_ref[...])
            vgroups.append([_pad_rows(new_ref[:, kh * 256 + 128:(kh + 1) * 256], LANE).astype(BF16)])
        _fa_update(jnp.concatenate(s, axis=0), vgroups, m_ref, l_ref, acc_ref)
        o = acc_ref[...] / l_ref[...]
        for kh in range(B_KVH):
            o_ref[kh] = o[kh * 16:(kh + 1) * 16]


def _attn_b_s(pt, qs, cqb, cache, layer, cst, new, tri, pps):
    n_seq, n_pages = pt.shape
    n_steps = n_pages // pps
    grid_spec = pltpu.PrefetchScalarGridSpec(
        num_scalar_prefetch=1,
        grid=(n_seq, n_steps),
        in_specs=[pl.BlockSpec((None, 4, 16, 128), lambda s, j, pt: (s, 0, 0, 0)),
                  pl.BlockSpec((None, 4, 16, 128), lambda s, j, pt: (s, 0, 0, 0))]
        + _kv_page_specs(layer, pps)
        + [pl.BlockSpec((None, 4, 2, pps * 128), lambda s, j, pt: (s, 0, 0, j)),
           pl.BlockSpec((None, 4, 2, 128), lambda s, j, pt: (s, 0, 0, n_pages)),
           pl.BlockSpec((None, 8, 1024), lambda s, j, pt: (s, 0, 0)),
           pl.BlockSpec(tri.shape, lambda s, j, pt: (0, 0))],
        out_specs=pl.BlockSpec((None, 4, 16, 128), lambda s, j, pt: (s, 0, 0, 0)),
        scratch_shapes=_fa_scratch(B_KVH * 16),
    )
    return pl.pallas_call(
        functools.partial(_attn_b_s_kernel, pps=pps, n_steps=n_steps),
        grid_spec=grid_spec,
        out_shape=jax.ShapeDtypeStruct((n_seq, 4, 16, 128), F32),
        compiler_params=_cp(("parallel", "arbitrary")),
        name="attn_b_s",
    )(pt, qs, cqb, *([cache] * (2 * pps)), cst, cst, new, tri)


def _nsa_s1_kernel(pt_ref, q_ref, *rest, pps, n_steps, n_cmp, n_sel, nsp, q0):
    pages = rest[:pps]
    pe_ref, w1_ref, w2_ref, kn_ref, bc_ref, ocmp_ref, sel_ref, xk_ref, xv_ref, flat_ref = rest[pps:]
    j = pl.program_id(1)
    for u, w in enumerate(pages):
        r0 = pl.multiple_of((j * pps + u) * PAGE, PAGE)
        for kh in range(C_KVH):
            xk_ref[kh, pl.ds(r0, PAGE), :] = _sub(w, 4 * kh)
            xv_ref[kh, pl.ds(r0, PAGE), :] = _sub(w, 4 * kh + 1)

    @pl.when(j == n_steps - 1)
    def _():
        for kh in range(C_KVH):
            kc, vc = _compress((xk_ref.at[kh], xv_ref.at[kh]), flat_ref, pe_ref, w1_ref, w2_ref, kn_ref, n_cmp)
            q = q_ref[kh]
            qg = [q[g * 8:(g + 1) * 8] for g in range(C_GRP)]
            ocmp, sel = _cmp_and_select(qg, kc.astype(BF16), vc.astype(BF16),
                                        lambda g, kh=kh: bc_ref[kh, g * 8:(g + 1) * 8], q0, 8, nsp, n_sel)
            for g in range(C_GRP):
                ocmp_ref[kh, g * 8:(g + 1) * 8, :] = ocmp[g]
            sel_ref[kh] = sel


def _nsa_s1(pt, qs, cache, layer, pe, w1, w2, kn0, bias_cmp, pps, n_sel, nsp, q0):
    n_seq, n_pages = pt.shape
    n_steps = n_pages // pps
    n_cmp = n_pages * PAGE // L_BLK
    whole = lambda a: pl.BlockSpec(a.shape, lambda s, j, pt: (0,) * a.ndim)
    per_s = lambda r, c: pl.BlockSpec((None, C_KVH, r, c), lambda s, j, pt: (s, 0, 0, 0))
    grid_spec = pltpu.PrefetchScalarGridSpec(
        num_scalar_prefetch=1,
        grid=(n_seq, n_steps),
        in_specs=[per_s(32, 128)] + _page_specs(layer, pps)
        + [whole(pe), whole(w1), whole(w2), whole(kn0), whole(bias_cmp)],
        out_specs=[per_s(32, 128), per_s(8, nsp)],
        scratch_shapes=[pltpu.VMEM((C_KVH, n_pages * PAGE, 128), F32), pltpu.VMEM((C_KVH, n_pages * PAGE, 128), F32),
                        pltpu.VMEM((n_cmp, L_BLK * C_HD), BF16)],
    )
    return pl.pallas_call(
        functools.partial(_nsa_s1_kernel, pps=pps, n_steps=n_steps, n_cmp=n_cmp, n_sel=n_sel, nsp=nsp, q0=q0),
        grid_spec=grid_spec,
        out_shape=[jax.ShapeDtypeStruct((n_seq, C_KVH, 32, 128), F32),
                   jax.ShapeDtypeStruct((n_seq, C_KVH, 8, nsp), F32)],
        compiler_params=_cp(("parallel", "arbitrary")),
        name="nsa_s1",
    )(pt, qs, *([cache] * pps), pe, w1, w2, kn0, bias_cmp)


def _nsa_s2_kernel(pt_ref, q_ref, *rest, pps, n_steps):
    pages = rest[:pps]
    (am_ref, amn_ref, new_ref, wst_ref, wnew_ref, blast_ref, bnew_ref, cfar_ref, bwin_ref, gt_ref, ocmp_ref,
     o_ref, m_ref, l_ref, acc_ref) = rest[pps:]
    j = pl.program_id(1)
    last = j == n_steps - 1

    @pl.when(j == 0)
    def _():
        _fa_init(m_ref, l_ref, acc_ref)

    s, vgroups = [], []
    for kh in range(C_KVH):
        q = q_ref[kh]
        far = cfar_ref[kh]
        row = []
        for u, w in enumerate(pages):
            bias = jnp.where(last, blast_ref[kh], far) if u == pps - 1 else far
            row.append(_dot_t(q, _sub(w, 4 * kh + 2).astype(BF16)) + bias
                       + _rows(am_ref[kh, :, u * LANE:(u + 1) * LANE], C_GRP))
        s.append(jnp.concatenate(row, axis=1))
        vgroups.append([_sub(w, 4 * kh + 3).astype(BF16) for w in pages])
    _fa_update(jnp.concatenate(s, axis=0), vgroups, m_ref, l_ref, acc_ref)

    @pl.when(last)
    def _():
        s, vgroups = [], []
        for kh in range(C_KVH):
            c0 = kh * 512 + 256
            kn = _pad_rows(new_ref[:, c0:c0 + 128], LANE).astype(BF16)
            s.append(_dot_t(q_ref[kh], kn) + bnew_ref[kh] + _rows(amn_ref[kh], C_GRP))
            vgroups.append([_pad_rows(new_ref[:, c0 + 128:c0 + 256], LANE).astype(BF16)])
        _fa_update(jnp.concatenate(s, axis=0), vgroups, m_ref, l_ref, acc_ref)
        o_sel_all = acc_ref[...] / l_ref[...]
        for kh in range(C_KVH):
            q = q_ref[kh]
            o_sel = o_sel_all[kh * 32:(kh + 1) * 32]
            w0 = kh * 256
            wk = wst_ref[:, w0:w0 + 128].astype(BF16)
            wv = wst_ref[:, w0 + 128:w0 + 256].astype(BF16)
            wkn = _pad_rows(wnew_ref[:, w0:w0 + 128], LANE).astype(BF16)
            wvn = _pad_rows(wnew_ref[:, w0 + 128:w0 + 256], LANE).astype(BF16)
            nst = wk.shape[0]
            sw = jnp.concatenate([_dot_t(q, wk) + bwin_ref[kh], _dot_t(q, wkn) + bnew_ref[kh]], axis=1)
            p = jnp.exp(sw - jnp.max(sw, axis=-1, keepdims=True))
            o_win = (jnp.dot(p[:, 0:nst].astype(BF16), wv, preferred_element_type=F32)
                     + jnp.dot(p[:, nst:].astype(BF16), wvn, preferred_element_type=F32)) / jnp.sum(p, axis=-1, keepdims=True)
            gt = gt_ref[kh]
            o_ref[kh] = gt[:, 0:128] * ocmp_ref[kh] + gt[:, 128:256] * o_sel + gt[:, 256:384] * o_win


def _nsa_s2(pt, qs, cache, layer, addmask, c_new8, win_state, w_new8, blast, bnew, cfar, bwin, gates, ocmp, pps):
    n_seq, n_pages = pt.shape
    n_steps = n_pages // pps
    nst = win_state.shape[2]
    whole = lambda a: pl.BlockSpec(a.shape, lambda s, j, pt: (0,) * a.ndim)
    per_s = lambda r, c: pl.BlockSpec((None, C_KVH, r, c), lambda s, j, pt: (s, 0, 0, 0))
    grid_spec = pltpu.PrefetchScalarGridSpec(
        num_scalar_prefetch=1,
        grid=(n_seq, n_steps),
        in_specs=[per_s(32, 128)] + _page_specs(layer, pps)
        + [pl.BlockSpec((None, C_KVH, 8, pps * 128), lambda s, j, pt: (s, 0, 0, j)),
           pl.BlockSpec((None, C_KVH, 8, 128), lambda s, j, pt: (s, 0, 0, n_pages)),
           pl.BlockSpec((None, 8, c_new8.shape[2]), lambda s, j, pt: (s, 0, 0)),
           pl.BlockSpec((None, None, nst, win_state.shape[3]), lambda s, j, pt: (layer, s, 0, 0)),
           pl.BlockSpec((None, 8, w_new8.shape[2]), lambda s, j, pt: (s, 0, 0)),
           whole(blast), whole(bnew), whole(cfar), whole(bwin), per_s(32, 384), per_s(32, 128)],
        out_specs=per_s(32, 128),
        scratch_shapes=_fa_scratch(C_KVH * 32),
    )
    return pl.pallas_call(
        functools.partial(_nsa_s2_kernel, pps=pps, n_steps=n_steps),
        grid_spec=grid_spec,
        out_shape=jax.ShapeDtypeStruct((n_seq, C_KVH, 32, 128), F32),
        compiler_params=_cp(("parallel", "arbitrary")),
        name="nsa_s2",
    )(pt, qs, *([cache] * pps), addmask, addmask, c_new8, win_state, w_new8, blast, bnew, cfar, bwin, gates, ocmp)


def _final_kernel(x_ref, oa_ref, ob_ref, oc_ref, ga_ref, gb_ref, gc_ref, mg_ref, sub_ref, wb_ref, wo_ref, y_ref,
                  *, c_sub, d):
    y = None
    for n, (o_ref, g_ref) in enumerate(((oa_ref, ga_ref), (ob_ref, gb_ref), (oc_ref, gc_ref))):
        parts = []
        for h in range(8):
            sl = slice(h * LANE, (h + 1) * LANE)
            o = o_ref[:, sl]
            if n == 0:
                o = _rms(o, sub_ref[...]) * c_sub
            g = g_ref[:, sl]
            parts.append((o * (g * _sigmoid(g))).astype(BF16))
        proj = jnp.dot(jnp.concatenate(parts, axis=1), wb_ref[n], preferred_element_type=F32)
        term = _sigmoid(mg_ref[:, n * d:(n + 1) * d]) * proj
        y = term if y is None else y + term
    y_ref[...] = x_ref[...] + jnp.dot(y.astype(BF16), wo_ref[...], preferred_element_type=F32)


def _final(x, oa, ob, oc, z, sub, wb, wo, c_sub):
    m, d = x.shape
    tm = min(m, 128)
    row = lambda w, blk=0: pl.BlockSpec((tm, w), lambda i, blk=blk: (i, blk))
    once = lambda a: pl.BlockSpec(a.shape, lambda i: (0,) * a.ndim, pipeline_mode=pl.Buffered(1))
    return pl.pallas_call(
        functools.partial(_final_kernel, c_sub=c_sub, d=d),
        grid=(m // tm,),
        in_specs=[row(d), row(1024), row(1024), row(1024), row(1024, BLK_AG), row(1024, BLK_BG), row(1024, BLK_CG),
                  row(3 * d, 0), once(sub), once(wb), once(wo)],
        out_specs=row(d),
        out_shape=jax.ShapeDtypeStruct((m, d), F32),
        compiler_params=_cp(("parallel",), 56),
        name="final",
    )(x, oa, ob, oc, z, z, z, z, sub, wb, wo)


def _bucket_np(dist):
    n = np.maximum(dist, 0)
    exact = REL_BUCKETS // 2
    nf = np.maximum(n, exact).astype(np.float32)
    big = exact + (np.log(nf / np.float32(exact)) / np.float32(math.log(REL_MAX_DIST / exact))
                   * np.float32(REL_BUCKETS - exact)).astype(np.int32)
    return np.where(n < exact, n, np.minimum(big, REL_BUCKETS - 1)).astype(np.int32)


def _rel_table(rel, dist, valid):
    tab = jnp.take(rel.astype(F32), jnp.asarray(_bucket_np(dist)), axis=0)
    tab = jnp.where(jnp.asarray(valid)[..., None], tab, NEG)
    return jnp.moveaxis(tab, -1, 0)


def _toeplitz(rel, off, n, valid):
    d = off + np.arange(n - 1, -n, -1)
    w = _rel_table(rel, d, valid(d))
    w = jnp.pad(w, ((0, 0), (0, 1)))
    skew = jnp.tile(w, (1, n))[:, :n * (2 * n - 1)].reshape(-1, n, 2 * n - 1)
    return skew[:, :, n - 1:]


def _prompt_tables(rel, tq, with_window):
    tabs = [_toeplitz(rel, 0, tq, lambda d: d >= 0), _toeplitz(rel, tq, tq, lambda d: d >= 0)]
    if with_window:
        tabs.append(_toeplitz(rel, WINDOW, tq, lambda d: d <= WINDOW))
    return jnp.stack(tabs, axis=1)


def _cmp_table(rel, t, n_cmp, ncp):
    d0 = L_BLK * (n_cmp - 1) + L_BLK - 1
    d = np.arange(-d0, t)
    f = _rel_table(rel, d, d >= 0)
    cols = [lax.slice_in_dim(f, d0 - L_BLK * j - (L_BLK - 1), d0 - L_BLK * j - (L_BLK - 1) + t, axis=1)
            for j in range(n_cmp)]
    tab = jnp.stack(cols, axis=2)
    return jnp.pad(tab, ((0, 0), (0, 0), (0, ncp - n_cmp)), constant_values=NEG)


def _layer_weights(l, rms_gain, w_in, b_forget, a_q_norm, a_k_norm, a_lambda, a_subln, b_q_norm, b_k_norm, c_q_norm,
                   c_k_norm, c_cmp_pe, c_cmp_w1, c_cmp_w2, w_branch, w_out):
    w = w_in[l]
    d = w.shape[0]
    w_perm = jnp.concatenate([w[:, a:b] for a, b in _PERM] + [jnp.zeros((d, N_PAD - N_USED + 96), F32)],
                             axis=1).astype(BF16)
    two = lambda g: jnp.concatenate([g, g])
    pn = jnp.stack([two(a_q_norm[l]), two(a_k_norm[l]), b_q_norm[l], b_k_norm[l], c_q_norm[l],
                    c_k_norm[l, 0], c_k_norm[l, 1], c_k_norm[l, 2]]).astype(F32)
    bf = jnp.zeros((1, LANE), F32).at[0, :B_HEADS].set(b_forget[l])
    al = a_lambda[l].astype(F32)
    lam_init = 0.8 - 0.6 * math.exp(-0.3 * l)
    lam = (jnp.exp(jnp.sum(al[0] * al[1])) - jnp.exp(jnp.sum(al[2] * al[3])) + lam_init).reshape(1)
    return dict(g=rms_gain[l], w=w_perm, pn=pn, bf=bf, lam=lam, c_sub=1.0 - lam_init,
                sub=a_subln[l].reshape(1, LANE), pe=c_cmp_pe[l], w1=c_cmp_w1[l].astype(BF16),
                w2=c_cmp_w2[l].astype(BF16), kn0=c_k_norm[l, 0].reshape(1, LANE),
                wb=w_branch[l].astype(BF16), wo=w_out[l].astype(BF16))


def _prompt_layer(x, lw, tb, tq):
    bsz, t, d = x.shape
    m = bsz * t
    z = _proj(x.reshape(m, d), lw["g"], lw["w"])
    qa, qb, qc, a_new, b_new, c_new, w_new, sm = _post(z, lw["pn"], lw["bf"])
    r3 = lambda a: a.reshape(bsz, t, a.shape[-1])
    qa, qb, qc, a_new, b_new, c_new, w_new, sm = map(r3, (qa, qb, qc, a_new, b_new, c_new, w_new, sm))
    logf = sm[..., :B_HEADS]
    o_a = _attn_a(qa, a_new, tb["a_tabs"], tb["a_far"], lw["lam"], tq)
    csum = jnp.cumsum(logf, axis=1).reshape(bsz, t, B_KVH, 2)
    o_b = _attn_b(qb, b_new, jnp.transpose(csum, (0, 2, 1, 3)), jnp.transpose(csum, (0, 2, 3, 1)), tb["tri"], tq)
    kc, vc = _cmp_p(c_new, lw["pe"], lw["w1"], lw["w2"], lw["kn0"], tb["ncp"])
    gates = jnp.transpose(sm[..., 8:8 + 3 * C_HEADS].reshape(bsz, t, C_KVH, 3 * C_GRP), (0, 2, 1, 3))
    o_c = _nsa_p(qc, kc, vc, c_new, w_new, gates, tb["c_cmp"], tb["c_tabs"], tb["expand"], tb["c_far"], tq)
    y = _final(x.reshape(m, d), o_a.reshape(m, 1024), o_b.reshape(m, 1024), o_c.reshape(m, 1024), z,
               lw["sub"], lw["wb"], lw["wo"], lw["c_sub"])
    keep = min(WINDOW, t)
    return (y.reshape(bsz, t, d), a_new.reshape(bsz, t, A_KVH, 256), b_new.reshape(bsz, t, B_KVH, 256), logf,
            c_new.reshape(bsz, t, C_KVH, 4, C_HD), w_new[:, t - keep:].reshape(bsz, keep, C_KVH, 2, C_HD))


def _stack_q(q, n_seq, t, kvh, grp):
    q = q.reshape(n_seq, t, kvh, grp, LANE)
    q = jnp.pad(q, ((0, 0), (0, 8 - t), (0, 0), (0, 0), (0, 0)))
    return jnp.transpose(q, (0, 2, 3, 1, 4)).reshape(n_seq, kvh, grp * 8, LANE)


def _unstack_o(o, n_seq, t, kvh, grp):
    o = o.reshape(n_seq, kvh, grp, 8, LANE)[:, :, :, :t]
    return jnp.transpose(o, (0, 3, 1, 2, 4)).reshape(n_seq * t, kvh * grp * LANE)


def _sample_layer(x, l, lw, tb, caches, page_table, pps):
    n_seq, t, d = x.shape
    m = n_seq * t
    cache_a, cache_b, cache_lf, cache_c, win_state = caches
    n_pages = page_table.shape[1]
    past = n_pages * PAGE
    z = _proj(x.reshape(m, d), lw["g"], lw["w"])
    qa, qb, qc, a_new, b_new, c_new, w_new, sm = _post(z, lw["pn"], lw["bf"])
    logf = sm[:, :B_HEADS].reshape(n_seq, t, B_HEADS)
    pad8 = lambda a: jnp.pad(a.reshape(n_seq, t, a.shape[-1]), ((0, 0), (0, 8 - t), (0, 0)))
    a_new8, b_new8, c_new8, w_new8 = map(pad8, (a_new, b_new, c_new, w_new))
    qa5 = qa.reshape(n_seq, t, A_KVH, 2, 2, A_HD)
    zq = jnp.zeros_like(qa5[..., 0, :])
    qa_m = jnp.stack([jnp.concatenate([qa5[..., 0, :], zq], -1), jnp.concatenate([zq, qa5[..., 1, :]], -1)], axis=4)
    qs_a = _stack_q(qa_m.reshape(m, A_KVH * 4 * LANE), n_seq, t, A_KVH, 4)
    o_a = _attn_a_s(page_table, lw["lam"], qs_a, cache_a, l, a_new8, tb["a_last"], tb["a_new"], tb["a_far"], pps)
    o_a = _unstack_o(o_a, n_seq, t, A_KVH, 2)
    lf_past = cache_lf[l][page_table].reshape(n_seq, past, B_HEADS).astype(F32)
    csum = jnp.cumsum(jnp.concatenate([lf_past, logf], axis=1), axis=1)
    cst = jnp.pad(jnp.transpose(csum, (0, 2, 1)), ((0, 0), (0, 0), (0, 128 - t))).reshape(n_seq, B_KVH, 2, past + 128)
    cq = jnp.pad(csum[:, past:], ((0, 0), (0, 8 - t), (0, 0))).reshape(n_seq, 8, B_KVH, 2)
    cqb = jnp.broadcast_to(jnp.transpose(cq, (0, 2, 3, 1)).reshape(n_seq, B_KVH, 16, 1), (n_seq, B_KVH, 16, LANE))
    o_b = _attn_b_s(page_table, _stack_q(qb, n_seq, t, B_KVH, 2), cqb, cache_b, l, cst, b_new8, tb["tri"], pps)
    o_b = _unstack_o(o_b, n_seq, t, B_KVH, 2)
    qs_c = _stack_q(qc, n_seq, t, C_KVH, C_GRP)
    n_sel = -(-(past + t) // L_BLK)
    ocmp, sel = _nsa_s1(page_table, qs_c, cache_c, l, lw["pe"], lw["w1"], lw["w2"], lw["kn0"], tb["c_cmp"], pps,
                        n_sel, tb["nsp"], past)
    selx = jnp.repeat(sel[..., :n_sel], L_BLK, axis=-1)
    addmask = (jnp.pad(selx, ((0, 0),) * 3 + ((0, past + 128 - n_sel * L_BLK),)) - 1.0) * (-NEG)
    g = jnp.pad(sm[:, 8:8 + 3 * C_HEADS].reshape(n_seq, t, C_KVH, C_GRP, 3), ((0, 0), (0, 8 - t)) + ((0, 0),) * 3)
    g = jnp.transpose(g, (0, 2, 4, 3, 1)).reshape(n_seq, C_KVH, 3, 32, 1)
    gates = jnp.transpose(jnp.broadcast_to(g, (n_seq, C_KVH, 3, 32, LANE)), (0, 1, 3, 2, 4)).reshape(n_seq, C_KVH, 32, 384)
    o_c = _nsa_s2(page_table, qs_c, cache_c, l, addmask, c_new8, win_state, w_new8, tb["c_last"], tb["c_new"],
                  tb["c_far"], tb["c_win"], gates, ocmp, pps)
    o_c = _unstack_o(o_c, n_seq, t, C_KVH, C_GRP)
    y = _final(x.reshape(m, d), o_a, o_b, o_c, z, lw["sub"], lw["wb"], lw["wo"], lw["c_sub"])
    w_all = jnp.concatenate([win_state[l].reshape(n_seq, -1, C_KVH, 2, C_HD), w_new.reshape(n_seq, t, C_KVH, 2, C_HD)], 1)
    keep = win_state.shape[2]
    return (y.reshape(n_seq, t, d), a_new.reshape(n_seq, t, A_KVH, 256), b_new.reshape(n_seq, t, B_KVH, 256), logf,
            c_new.reshape(n_seq, t, C_KVH, 4, C_HD), w_all[:, w_all.shape[1] - keep:])


def _sample_tables(rel_a, rel_c, t, past, n_cmp, nst):
    tok = np.arange(8)
    jj = np.arange(LANE)
    d_last = LANE + tok[:, None] - jj[None, :]
    d_new = tok[:, None] - jj[None, :]
    v_new = (d_new >= 0) & (jj[None, :] < t)
    ones = np.ones((8, LANE), bool)

    def rows(rel, dist, valid, kvh, per):
        tab = _rel_table(rel, dist, valid)
        return tab.reshape(kvh, per * 8, dist.shape[1])

    tb = dict(a_last=rows(rel_a, d_last, ones, A_KVH, 4), a_new=rows(rel_a, d_new, v_new, A_KVH, 4),
              c_last=rows(rel_c, d_last, ones, C_KVH, C_GRP), c_new=rows(rel_c, d_new, v_new, C_KVH, C_GRP))
    far = lambda rel, kvh, per: jnp.broadcast_to(
        jnp.repeat(rel[REL_BUCKETS - 1].astype(F32), 8).reshape(kvh, per * 8, 1), (kvh, per * 8, LANE))
    tb["a_far"] = far(rel_a, A_KVH, 4)
    tb["c_far"] = far(rel_c, C_KVH, C_GRP)
    ws = np.arange(nst)
    d_win = nst + tok[:, None] - ws[None, :]
    tb["c_win"] = rows(rel_c, d_win, (d_win >= 0) & (d_win <= WINDOW), C_KVH, C_GRP)
    d_cmp = past + tok[:, None] - (L_BLK * np.arange(n_cmp)[None, :] + L_BLK - 1)
    tb["c_cmp"] = rows(rel_c, d_cmp, d_cmp >= 0, C_KVH, C_GRP)
    tri = np.where(v_new, 0.0, NEG).astype(np.float32)
    tb["tri"] = jnp.asarray(np.concatenate([tri, tri], axis=0))
    return tb


def kernel(x_prompt, x_sample, cache_a_kv, cache_b_kv, cache_b_logf, cache_c_kv, state_c_win, page_table, rms_gain,
           w_in, b_forget, a_q_norm, a_k_norm, a_lambda, a_subln, b_q_norm, b_k_norm, c_q_norm, c_k_norm, c_cmp_pe,
           c_cmp_w1, c_cmp_w2, rel_bias, w_branch, w_out):
    depth = w_in.shape[0]
    bsz, t, d = x_prompt.shape
    n_seq, t_s, _ = x_sample.shape
    n_pages = page_table.shape[1]
    past = n_pages * cache_a_kv.shape[2]
    nst = state_c_win.shape[2]
    assert cache_a_kv.shape[2] == PAGE and past % L_BLK == 0 and t_s <= 8 and nst == WINDOW and past >= WINDOW
    tq = min(256, t)
    assert t % tq == 0 and WINDOW % tq == 0 and WINDOW // tq >= 2 and tq % L_BLK == 0
    pps = 8
    assert n_pages % pps == 0

    rel_a = rel_bias[:, :2 * A_HEADS]
    rel_c = rel_bias[:, 2 * A_HEADS:]
    n_cmp_p = t // L_BLK
    ncp = -(-n_cmp_p // LANE) * LANE
    nsp_p = -(-(-(-t // L_BLK)) // LANE) * LANE
    tri = np.where(np.arange(tq)[:, None] >= np.arange(tq)[None, :], 0.0, NEG).astype(np.float32)
    expand = (np.arange(t)[None, :] // L_BLK == np.arange(nsp_p)[:, None]).astype(np.float32)
    tb_p = dict(
        a_tabs=_prompt_tables(rel_a, tq, False).reshape(A_KVH, 4, 2, tq, tq).transpose(0, 2, 1, 3, 4)
        .reshape(A_KVH, 2, 4 * tq, tq),
        a_far=rel_a[REL_BUCKETS - 1].astype(F32),
        c_tabs=_prompt_tables(rel_c, tq, True).reshape(C_KVH, C_GRP, 3, tq, tq).transpose(0, 2, 1, 3, 4)
        .reshape(C_KVH, 3, C_GRP * tq, tq),
        c_far=rel_c[REL_BUCKETS - 1].astype(F32),
        c_cmp=_cmp_table(rel_c, t, n_cmp_p, ncp),
        tri=jnp.asarray(np.concatenate([tri, tri], axis=0)), expand=jnp.asarray(expand, dtype=BF16), ncp=ncp)
    n_sel_s = -(-(past + t_s) // L_BLK)
    tb_s = _sample_tables(rel_a, rel_c, t_s, past, past // L_BLK, nst)
    tb_s["nsp"] = -(-n_sel_s // LANE) * LANE

    caches = (cache_a_kv, cache_b_kv, cache_b_logf,
              cache_c_kv.reshape(depth, cache_c_kv.shape[1], PAGE * SUBROWS, LANE),
              state_c_win.reshape(depth, n_seq, nst, C_KVH * 2 * C_HD))
    yp, ys = x_prompt, x_sample
    st_p, st_s = [], []
    for l in range(depth):
        lw = _layer_weights(l, rms_gain, w_in, b_forget, a_q_norm, a_k_norm, a_lambda, a_subln, b_q_norm, b_k_norm,
                            c_q_norm, c_k_norm, c_cmp_pe, c_cmp_w1, c_cmp_w2, w_branch, w_out)
        yp, *sp = _prompt_layer(yp, lw, tb_p, tq)
        ys, *ss = _sample_layer(ys, l, lw, tb_s, caches, page_table, pps)
        st_p.append(sp)
        st_s.append(ss)
    stk = lambda lst, i: jnp.stack([s[i] for s in lst])
    return (yp, ys, stk(st_p, 0), stk(st_s, 0), stk(st_p, 1), stk(st_s, 1), stk(st_p, 2), stk(st_s, 2),
            stk(st_p, 3), stk(st_s, 3), stk(st_p, 4), stk(st_s, 4))
```

```python
import functools
import math

import numpy as np
import jax
import jax.numpy as jnp
from jax import lax
from jax.experimental import pallas as pl
from jax.experimental.pallas import tpu as pltpu

F32 = jnp.float32
BF16 = jnp.bfloat16

A_HD, A_VD, A_HEADS, A_KVH = 64, 128, 8, 4
B_HD, B_HEADS, B_KVH = 128, 8, 4
C_HD, C_HEADS, C_KVH, C_GRP = 128, 8, 2, 4
L_BLK, N_SEL, WINDOW = 64, 16, 512
REL_BUCKETS, REL_MAX_DIST = 32, 128
EPS, NEG, FORCE = 1e-6, -1e30, 1e6
SCALE_A, SCALE_BC = A_HD ** -0.5, 128 ** -0.5
LANE = 128
PAGE = 128
VMEM_MB = 48

_PERM = [(9760, 15904), (2048, 3072), (5128, 6152), (8736, 9760),
         (0, 1024), (3072, 4096), (6152, 7176), (1024, 1536), (1536, 2048), (4096, 4608), (4608, 5120),
         (7176, 8712), (5120, 5128), (8712, 8736)]
N_USED = 15904
TN = 768
N_GATE = 9216
N_REST = 6912
N_PAD = N_GATE + N_REST
BLK_AG, BLK_BG, BLK_CG = 6, 7, 8
BLK_AQ, BLK_BQ, BLK_CQ = 0, 1, 2
BLK_AK, BLK_AV, BLK_BK, BLK_BV, BLK_C0, BLK_C1, BLK_C2 = 6, 7, 8, 9, 10, 11, 12
BLK_SM = 52


def _cp(sem, mb=VMEM_MB):
    return pltpu.CompilerParams(dimension_semantics=sem, vmem_limit_bytes=mb * 1024 * 1024)


def _smem():
    return pl.BlockSpec(memory_space=pltpu.SMEM)


def _rms(x, g):
    ms = jnp.mean(x * x, axis=-1, keepdims=True)
    return x * lax.rsqrt(ms + EPS) * g


def _rms_half(x, g2):
    sq = x * x
    lo = lax.broadcasted_iota(jnp.int32, x.shape, 1) < 64
    s_lo = jnp.sum(jnp.where(lo, sq, 0.0), axis=-1, keepdims=True)
    s_hi = jnp.sum(jnp.where(lo, 0.0, sq), axis=-1, keepdims=True)
    ms = jnp.where(lo, s_lo, s_hi) * (1.0 / 64)
    return x * lax.rsqrt(ms + EPS) * g2


def _sigmoid(x):
    return 1.0 / (1.0 + jnp.exp(-x))


def _dot_t(a, b):
    return lax.dot_general(a, b, (((1,), (1,)), ((), ())), preferred_element_type=F32)


def _lanes(x, n):
    return x if n == LANE else jnp.concatenate([x] * (n // LANE), axis=1)


def _rows(x, times):
    return x if times == 1 else jnp.concatenate([x] * times, axis=0)


def _fa_update(s, vgroups, m_ref, l_ref, acc_ref, shift=None):
    m_prev = m_ref[...]
    rmax = jnp.max(s, axis=-1, keepdims=True)
    if shift is not None:
        rmax = rmax + shift
    m_new = jnp.maximum(m_prev, rmax)
    alpha = jnp.exp(m_prev - m_new)
    off = m_new if shift is None else m_new - shift
    p = jnp.exp(s - _lanes(off, s.shape[1]))
    l_ref[...] = alpha * l_ref[...] + jnp.sum(p, axis=-1, keepdims=True)
    pb = p.astype(BF16)
    rg = s.shape[0] // len(vgroups)
    pv = []
    for gi, vs in enumerate(vgroups):
        lo, part = 0, None
        for v in vs:
            d = jnp.dot(pb[gi * rg:(gi + 1) * rg, lo:lo + v.shape[0]], v, preferred_element_type=F32)
            part = d if part is None else part + d
            lo += v.shape[0]
        pv.append(part)
    acc_ref[...] = alpha * acc_ref[...] + (pv[0] if len(pv) == 1 else jnp.concatenate(pv, axis=0))
    m_ref[...] = m_new


def _fa_init(m_ref, l_ref, acc_ref):
    m_ref[...] = jnp.full(m_ref.shape, NEG, F32)
    l_ref[...] = jnp.zeros(l_ref.shape, F32)
    acc_ref[...] = jnp.zeros(acc_ref.shape, F32)


def _fa_scratch(rows):
    return [pltpu.VMEM((rows, LANE), F32)] * 3


def _proj_kernel(x_ref, g_ref, w_ref, zg_ref, zr_ref, h_ref, *, gate_tiles):
    j = pl.program_id(1)

    @pl.when(j == 0)
    def _():
        h_ref[...] = _rms(x_ref[...], g_ref[...]).astype(BF16)

    acc = jnp.dot(h_ref[...], w_ref[...], preferred_element_type=F32)

    @pl.when(j < gate_tiles)
    def _():
        zg_ref[...] = acc.astype(BF16)

    @pl.when(j >= gate_tiles)
    def _():
        zr_ref[...] = acc


def _proj(x, g, w):
    m, d = x.shape
    tm = next(c for c in (1024, 512, 256, 128, m) if m % c == 0)
    gt = N_GATE // TN
    return pl.pallas_call(
        functools.partial(_proj_kernel, gate_tiles=gt),
        grid=(m // tm, N_PAD // TN),
        in_specs=[pl.BlockSpec((tm, d), lambda i, j: (i, 0)),
                  pl.BlockSpec((1, d), lambda i, j: (0, 0)),
                  pl.BlockSpec((d, TN), lambda i, j: (0, j))],
        out_specs=[pl.BlockSpec((tm, TN), lambda i, j: (i, jnp.minimum(j, gt - 1))),
                   pl.BlockSpec((tm, TN), lambda i, j: (i, jnp.maximum(j - gt, 0)))],
        out_shape=[jax.ShapeDtypeStruct((m, N_GATE), BF16), jax.ShapeDtypeStruct((m, N_REST), F32)],
        scratch_shapes=[pltpu.VMEM((tm, d), BF16)],
        compiler_params=_cp(("parallel", "arbitrary")),
        name="proj",
    )(x, g.reshape(1, d), w)


def _post_kernel(zaq, zbq, zcq, zak, zav, zbk, zbv, zc0, zc1, zc2, zsm, pn, bf,
                 qa, qb, qc, anew, bnew, cnew, wnew, sm):
    for h in range(8):
        sl = slice(h * LANE, (h + 1) * LANE)
        qa[:, sl] = (_rms_half(zaq[:, sl], pn[0:1]) * SCALE_A).astype(BF16)
        qb[:, sl] = (_rms(zbq[:, sl], pn[2:3]) * SCALE_BC).astype(BF16)
        qc[:, sl] = (_rms(zcq[:, sl], pn[4:5]) * SCALE_BC).astype(BF16)
    for k in range(4):
        src = slice(k * LANE, (k + 1) * LANE)
        anew[:, k * 256:k * 256 + 128] = _rms_half(zak[:, src], pn[1:2])
        anew[:, k * 256 + 128:(k + 1) * 256] = zav[:, src]
        bnew[:, k * 256:k * 256 + 128] = _rms(zbk[:, src], pn[3:4])
        bnew[:, k * 256 + 128:(k + 1) * 256] = zbv[:, src]
    for k in range(2):
        s0 = slice(k * LANE, (k + 1) * LANE)
        s1 = slice(256 + k * LANE, 256 + (k + 1) * LANE)
        cnew[:, k * 512:k * 512 + 128] = zc0[:, s0]
        cnew[:, k * 512 + 128:k * 512 + 256] = zc0[:, s1]
        cnew[:, k * 512 + 256:k * 512 + 384] = _rms(zc1[:, s0], pn[6:7])
        cnew[:, k * 512 + 384:k * 512 + 512] = zc1[:, s1]
        wnew[:, k * 256:k * 256 + 128] = _rms(zc2[:, s0], pn[7:8])
        wnew[:, k * 256 + 128:k * 256 + 256] = zc2[:, s1]
    z = zsm[...]
    x = z + bf[...]
    logf = jnp.minimum(x, 0.0) - jnp.log1p(jnp.exp(-jnp.abs(x)))
    lane = lax.broadcasted_iota(jnp.int32, z.shape, 1)
    sm[...] = jnp.where(lane < 8, logf, _sigmoid(z))


def _post(z, pn, bf):
    m = z.shape[0]
    tm = min(m, 256)

    def zs(width, blk):
        return pl.BlockSpec((tm, width), lambda i, blk=blk: (i, blk))

    def os(width):
        return pl.BlockSpec((tm, width), lambda i: (i, 0))

    const = lambda a: pl.BlockSpec(a.shape, lambda i: (0, 0))
    return pl.pallas_call(
        _post_kernel,
        grid=(m // tm,),
        in_specs=[zs(1024, BLK_AQ), zs(1024, BLK_BQ), zs(1024, BLK_CQ), zs(512, BLK_AK), zs(512, BLK_AV),
                  zs(512, BLK_BK), zs(512, BLK_BV), zs(512, BLK_C0), zs(512, BLK_C1), zs(512, BLK_C2),
                  zs(128, BLK_SM), const(pn), const(bf)],
        out_specs=[os(1024), os(1024), os(1024), os(1024), os(1024), os(1024), os(512), os(128)],
        out_shape=[jax.ShapeDtypeStruct((m, 1024), BF16)] * 3
        + [jax.ShapeDtypeStruct((m, 1024), F32)] * 3
        + [jax.ShapeDtypeStruct((m, 512), F32), jax.ShapeDtypeStruct((m, 128), F32)],
        compiler_params=_cp(("parallel",)),
        name="post",
    )(*([z] * 11), pn, bf)


def _attn_a_kernel(cfar_ref, lam_ref, q_ref, kv_ref, tab_ref, o_ref, m_ref, l_ref, acc_ref, cvec_ref, *, tq, tk):
    kh = pl.program_id(1)
    i = pl.program_id(2)
    q = q_ref[...]
    lo = lax.broadcasted_iota(jnp.int32, (tq, LANE), 1) < 64
    zero = jnp.zeros((tq, LANE), BF16)
    blocks = []
    for g in range(2):
        qh = q[:, g * LANE:(g + 1) * LANE]
        blocks.append(jnp.where(lo, qh, zero))
        blocks.append(jnp.where(lo, zero, qh))
    qs = jnp.concatenate(blocks, axis=0)
    for b in range(4):
        cvec_ref[b * tq:(b + 1) * tq, :] = jnp.full((tq, LANE), cfar_ref[kh * 4 + b], F32)
    _fa_init(m_ref, l_ref, acc_ref)

    def tile(kt, bias, shift):
        r0 = pl.multiple_of(kt * tk, tk)
        k = kv_ref[pl.ds(r0, tk), 0:128].astype(BF16)
        v = kv_ref[pl.ds(r0, tk), 128:256].astype(BF16)
        s = _dot_t(qs, k)
        if bias is not None:
            s = s + bias
        _fa_update(s, [[v]], m_ref, l_ref, acc_ref, shift)

    def far(kt, c):
        tile(kt, None, cvec_ref[...])
        return c

    lax.fori_loop(0, jnp.maximum(i - 1, 0), far, 0)

    @pl.when(i >= 1)
    def _():
        tile(i - 1, tab_ref[1], None)

    tile(i, tab_ref[0], None)
    lam = lam_ref[0]
    o = acc_ref[...] / l_ref[...]
    for g in range(2):
        o_ref[:, g * LANE:(g + 1) * LANE] = (o[2 * g * tq:(2 * g + 1) * tq]
                                             - lam * o[(2 * g + 1) * tq:(2 * g + 2) * tq]).astype(o_ref.dtype)


def _attn_a(qa, a_new, tabs, cfar, lam, tq):
    bsz, t, _ = qa.shape
    return pl.pallas_call(
        functools.partial(_attn_a_kernel, tq=tq, tk=tq),
        grid=(bsz, A_KVH, t // tq),
        in_specs=[_smem(), _smem(),
                  pl.BlockSpec((None, tq, 256), lambda b, k, i: (b, i, k)),
                  pl.BlockSpec((None, t, 256), lambda b, k, i: (b, 0, k)),
                  pl.BlockSpec((None, 2, 4 * tq, tq), lambda b, k, i: (k, 0, 0, 0))],
        out_specs=pl.BlockSpec((None, tq, 256), lambda b, k, i: (b, i, k)),
        out_shape=jax.ShapeDtypeStruct((bsz, t, 1024), BF16),
        scratch_shapes=_fa_scratch(4 * tq) + [pltpu.VMEM((4 * tq, LANE), F32)],
        compiler_params=_cp(("parallel", "parallel", "arbitrary")),
        name="attn_a",
    )(cfar, lam, qa, a_new, tabs)


def _attn_b_kernel(q_ref, kv_ref, cq_ref, ck_ref, tri_ref, o_ref, m_ref, l_ref, acc_ref, cqb_ref, *, tq, tk):
    i = pl.program_id(2)
    q = q_ref[...]
    qs = jnp.concatenate([q[:, 0:128], q[:, 128:256]], axis=0)
    for g in range(2):
        cqb_ref[g * tq:(g + 1) * tq, :] = jnp.broadcast_to(cq_ref[:, g:g + 1], (tq, LANE))
    _fa_init(m_ref, l_ref, acc_ref)

    def tile(kt, diag):
        r0 = pl.multiple_of(kt * tk, tk)
        k = kv_ref[pl.ds(r0, tk), 0:128].astype(BF16)
        v = kv_ref[pl.ds(r0, tk), 128:256].astype(BF16)
        ck = jnp.concatenate([jnp.broadcast_to(ck_ref[g:g + 1, pl.ds(r0, tk)], (tq, tk)) for g in range(2)], axis=0)
        s = _dot_t(qs, k) - ck
        if diag:
            s = s + tri_ref[...]
        _fa_update(s, [[v]], m_ref, l_ref, acc_ref, cqb_ref[...])

    def far(kt, c):
        tile(kt, False)
        return c

    lax.fori_loop(0, i, far, 0)
    tile(i, True)
    o = acc_ref[...] / l_ref[...]
    for g in range(2):
        o_ref[:, g * LANE:(g + 1) * LANE] = o[g * tq:(g + 1) * tq].astype(o_ref.dtype)


def _attn_b(qb, b_new, csq, cst, tri, tq):
    bsz, t, _ = qb.shape
    return pl.pallas_call(
        functools.partial(_attn_b_kernel, tq=tq, tk=tq),
        grid=(bsz, B_KVH, t // tq),
        in_specs=[pl.BlockSpec((None, tq, 256), lambda b, k, i: (b, i, k)),
                  pl.BlockSpec((None, t, 256), lambda b, k, i: (b, 0, k)),
                  pl.BlockSpec((None, None, tq, 2), lambda b, k, i: (b, k, i, 0)),
                  pl.BlockSpec((None, None, 2, t), lambda b, k, i: (b, k, 0, 0)),
                  pl.BlockSpec((2 * tq, tq), lambda b, k, i: (0, 0))],
        out_specs=pl.BlockSpec((None, tq, 256), lambda b, k, i: (b, i, k)),
        out_shape=jax.ShapeDtypeStruct((bsz, t, 1024), BF16),
        scratch_shapes=_fa_scratch(2 * tq) + [pltpu.VMEM((2 * tq, LANE), F32)],
        compiler_params=_cp(("parallel", "parallel", "arbitrary")),
        name="attn_b",
    )(qb, b_new, csq, cst, tri)


def _compress(x_refs, flat_ref, pe_ref, w1_ref, w2_ref, kn_ref, n_cmp):
    outs = []
    for c in range(2):
        for l in range(L_BLK):
            rows = x_refs[c][pl.ds(l, n_cmp, stride=L_BLK), :]
            flat_ref[:, l * LANE:(l + 1) * LANE] = (rows + pe_ref[c, l:l + 1, :]).astype(BF16)
        h = jnp.dot(flat_ref[...], w1_ref[c], preferred_element_type=F32)
        h = h * _sigmoid(h)
        o = jnp.dot(h.astype(BF16), w2_ref[c], preferred_element_type=F32)
        if c == 0:
            o = _rms(o, kn_ref[...])
        outs.append(o)
    return outs


def _cmp_and_select(qg, kc, vc, bias_of, q0, rows, nsp, n_sel):
    ncp = kc.shape[0]
    imp = jnp.zeros((rows, ncp), F32)
    ocmp = []
    for g, q in enumerate(qg):
        bias = bias_of(g)
        s = _dot_t(q, kc) + bias
        e = jnp.exp(s - jnp.max(s, axis=-1, keepdims=True))
        pc = jnp.where(bias > 0.5 * NEG, e / jnp.sum(e, axis=-1, keepdims=True), 0.0)
        imp = imp + pc
        ocmp.append(jnp.dot(pc.astype(BF16), vc, preferred_element_type=F32))
    if nsp > ncp:
        imp = jnp.concatenate([imp, jnp.zeros((rows, nsp - ncp), F32)], axis=1)
    j = lax.broadcasted_iota(jnp.int32, (rows, nsp), 1)
    pos = q0 + lax.broadcasted_iota(jnp.int32, (rows, nsp), 0)
    cur = lax.shift_right_logical(pos, 6)
    imp = imp + FORCE * (jnp.where(j == cur, 4.0, 0.0) + jnp.where(j == cur - 1, 2.0, 0.0)
                         + jnp.where(j == 0, 1.0, 0.0))
    imp = jnp.where(j <= cur, imp, NEG)
    cnt = jnp.zeros((rows, nsp), F32)
    for c in range(n_sel):
        col = imp[:, c:c + 1]
        first = jnp.where(j > c, 1.0, 0.0)
        cnt = cnt + jnp.where(col > imp, 1.0, jnp.where(col == imp, first, 0.0))
    sel = jnp.where(cnt < float(min(N_SEL, n_sel)), 1.0, 0.0)
    return ocmp, sel


def _cmp_p_kernel(xk_ref, xv_ref, pe_ref, w1_ref, w2_ref, kn_ref, kc_ref, vc_ref, flat_ref, *, n_cmp):
    kc, vc = _compress((xk_ref, xv_ref), flat_ref, pe_ref, w1_ref, w2_ref, kn_ref, n_cmp)
    kc_ref[...] = jnp.zeros(kc_ref.shape, BF16)
    vc_ref[...] = jnp.zeros(vc_ref.shape, BF16)
    kc_ref[0:n_cmp, :] = kc.astype(BF16)
    vc_ref[0:n_cmp, :] = vc.astype(BF16)


def _cmp_p(c_new, pe, w1, w2, kn0, ncp):
    bsz, t, _ = c_new.shape
    n_cmp = t // L_BLK
    whole = lambda a: pl.BlockSpec(a.shape, lambda b, k: (0,) * a.ndim)
    return pl.pallas_call(
        functools.partial(_cmp_p_kernel, n_cmp=n_cmp),
        grid=(bsz, C_KVH),
        in_specs=[pl.BlockSpec((None, t, 128), lambda b, k: (b, 0, 4 * k)),
                  pl.BlockSpec((None, t, 128), lambda b, k: (b, 0, 4 * k + 1)),
                  whole(pe), whole(w1), whole(w2), whole(kn0)],
        out_specs=[pl.BlockSpec((None, None, ncp, 128), lambda b, k: (b, k, 0, 0))] * 2,
        out_shape=[jax.ShapeDtypeStruct((bsz, C_KVH, ncp, 128), BF16)] * 2,
        scratch_shapes=[pltpu.VMEM((n_cmp, L_BLK * C_HD), BF16)],
        compiler_params=_cp(("parallel", "parallel")),
        name="nsa_cmp_p",
    )(c_new, c_new, pe, w1, w2, kn0)


def _nsa_p_kernel(cfar_ref, q_ref, kc_ref, vc_ref, sel_ref, win_ref, gt_ref, bc_ref, tab_ref, e_ref, o_ref,
                  ms_ref, ls_ref, accs_ref, mw_ref, lw_ref, accw_ref, cvec_ref, am_ref, *, tq, tk, n_sel, nsp, nw):
    kh = pl.program_id(1)
    i = pl.program_id(2)
    q = q_ref[...]
    qg = [q[:, g * LANE:(g + 1) * LANE] for g in range(C_GRP)]
    qs = jnp.concatenate(qg, axis=0)
    ocmp, sel = _cmp_and_select(qg, kc_ref[...], vc_ref[...], lambda g: bc_ref[g], i * tq, tq, nsp, n_sel)
    am_ref[...] = (jnp.dot(sel.astype(BF16), e_ref[...], preferred_element_type=F32) - 1.0) * (-NEG)
    for g in range(C_GRP):
        cvec_ref[g * tq:(g + 1) * tq, :] = jnp.full((tq, LANE), cfar_ref[kh * C_GRP + g], F32)
    _fa_init(ms_ref, ls_ref, accs_ref)
    _fa_init(mw_ref, lw_ref, accw_ref)

    def tile(src_ref, kt, bias, shift, masked, state):
        r0 = pl.multiple_of(kt * tk, tk)
        k = src_ref[pl.ds(r0, tk), 0:128].astype(BF16)
        v = src_ref[pl.ds(r0, tk), 128:256].astype(BF16)
        s = _dot_t(qs, k)
        if bias is not None:
            s = s + bias
        if masked:
            s = s + _rows(am_ref[:, pl.ds(r0, tk)], C_GRP)
        _fa_update(s, [[v]], *state, shift)

    sel_state = (ms_ref, ls_ref, accs_ref)
    win_state = (mw_ref, lw_ref, accw_ref)

    def far(kt, c):
        tile(sel_ref, kt, None, cvec_ref[...], True, sel_state)
        return c

    lax.fori_loop(0, jnp.maximum(i - 1, 0), far, 0)

    @pl.when(i >= 1)
    def _():
        tile(sel_ref, i - 1, tab_ref[1], None, True, sel_state)
        tile(win_ref, i - 1, tab_ref[1], None, False, win_state)

    tile(sel_ref, i, tab_ref[0], None, True, sel_state)
    tile(win_ref, i, tab_ref[0], None, False, win_state)
    for d in range(2, nw):
        @pl.when(i >= d)
        def _(d=d):
            tile(win_ref, i - d, None, cvec_ref[...], False, win_state)

    @pl.when(i >= nw)
    def _():
        tile(win_ref, i - nw, tab_ref[2], None, False, win_state)

    gt = gt_ref[...]
    o_sel = accs_ref[...] / ls_ref[...]
    o_win = accw_ref[...] / lw_ref[...]
    for g in range(C_GRP):
        rs = slice(g * tq, (g + 1) * tq)
        o_ref[:, g * LANE:(g + 1) * LANE] = (gt[:, 3 * g:3 * g + 1] * ocmp[g] + gt[:, 3 * g + 1:3 * g + 2] * o_sel[rs]
                                            + gt[:, 3 * g + 2:3 * g + 3] * o_win[rs]).astype(o_ref.dtype)


def _nsa_p(qc, kc, vc, c_new, w_new, gates, bias_cmp, tabs, expand, cfar, tq):
    bsz, t, _ = qc.shape
    ncp = kc.shape[2]
    n_sel = -(-t // L_BLK)
    nsp = expand.shape[0]
    kern = functools.partial(_nsa_p_kernel, tq=tq, tk=tq, n_sel=n_sel, nsp=nsp, nw=WINDOW // tq)
    rows = C_GRP * tq
    return pl.pallas_call(
        kern,
        grid=(bsz, C_KVH, t // tq),
        in_specs=[_smem(),
                  pl.BlockSpec((None, tq, 512), lambda b, k, i: (b, i, k)),
                  pl.BlockSpec((None, None, ncp, 128), lambda b, k, i: (b, k, 0, 0)),
                  pl.BlockSpec((None, None, ncp, 128), lambda b, k, i: (b, k, 0, 0)),
                  pl.BlockSpec((None, t, 256), lambda b, k, i: (b, 0, 2 * k + 1)),
                  pl.BlockSpec((None, t, 256), lambda b, k, i: (b, 0, k)),
                  pl.BlockSpec((None, None, tq, 12), lambda b, k, i: (b, k, i, 0)),
                  pl.BlockSpec((C_GRP, tq, ncp), lambda b, k, i: (k, i, 0)),
                  pl.BlockSpec((None, 3, rows, tq), lambda b, k, i: (k, 0, 0, 0)),
                  pl.BlockSpec(expand.shape, lambda b, k, i: (0, 0))],
        out_specs=pl.BlockSpec((None, tq, 512), lambda b, k, i: (b, i, k)),
        out_shape=jax.ShapeDtypeStruct((bsz, t, 1024), BF16),
        scratch_shapes=_fa_scratch(rows) + _fa_scratch(rows) + [pltpu.VMEM((rows, LANE), F32), pltpu.VMEM((tq, t), F32)],
        compiler_params=_cp(("parallel", "parallel", "arbitrary")),
        name="nsa_p",
    )(cfar, qc, kc, vc, c_new, w_new, gates, bias_cmp, tabs, expand)


SUBROWS = 8
PPS_ATTN = 16
PPS_CMP = 8


def _pad_rows(x, n):
    return jnp.concatenate([x, jnp.zeros((n - x.shape[0], x.shape[1]), x.dtype)], axis=0)


def _sub(page_ref, s):
    return page_ref[pl.ds(s, PAGE, stride=SUBROWS), :]


def _head_rows(win_ref, kh):
    return win_ref.reshape(4 * PAGE, LANE)[pl.ds(kh, PAGE, stride=4), :]


def _attn_a_s_kernel(pt_ref, lam_ref, q_ref, *rest, pps, n_steps):
    kwin = [lambda kh, w=w: _head_rows(w, kh) for w in rest[:pps]]
    vwin = [lambda kh, w=w: _head_rows(w, kh) for w in rest[pps:2 * pps]]
    new_ref, blast_ref, bnew_ref, cfar_ref, o_ref, m_ref, l_ref, acc_ref = rest[2 * pps:]
    j = pl.program_id(1)
    last = j == n_steps - 1

    @pl.when(j == 0)
    def _():
        _fa_init(m_ref, l_ref, acc_ref)

    s, vgroups = [], []
    for kh in range(A_KVH):
        q = q_ref[kh]
        far = cfar_ref[kh]
        row = [_dot_t(q, w(kh).astype(BF16)) + far for w in kwin[:-1]]
        row.append(_dot_t(q, kwin[-1](kh).astype(BF16)) + jnp.where(last, blast_ref[kh], far))
        s.append(jnp.concatenate(row, axis=1))
        vgroups.append([w(kh).astype(BF16) for w in vwin])
    _fa_update(jnp.concatenate(s, axis=0), vgroups, m_ref, l_ref, acc_ref)

    @pl.when(last)
    def _():
        lam = lam_ref[0]
        s, vgroups = [], []
        for kh in range(A_KVH):
            kn = _pad_rows(new_ref[:, kh * 256:kh * 256 + 128], LANE).astype(BF16)
            s.append(_dot_t(q_ref[kh], kn) + bnew_ref[kh])
            vgroups.append([_pad_rows(new_ref[:, kh * 256 + 128:(kh + 1) * 256], LANE).astype(BF16)])
        _fa_update(jnp.concatenate(s, axis=0), vgroups, m_ref, l_ref, acc_ref)
        o = acc_ref[...] / l_ref[...]
        for kh in range(A_KVH):
            for g in range(2):
                r = kh * 32 + 16 * g
                o_ref[kh, g * 8:(g + 1) * 8, :] = o[r:r + 8] - lam * o[r + 8:r + 16]


def _kv_page_specs(layer, pps):
    def spec(u, half):
        return pl.BlockSpec((None, None, PAGE, 4, LANE),
                            lambda s, j, pt: (layer, pt[s, j * pps + u], 0, 0, half))
    return [spec(u, 0) for u in range(pps)] + [spec(u, 1) for u in range(pps)]


def _page_specs(layer, pps):
    return [pl.BlockSpec((None, None, PAGE * SUBROWS, LANE),
                         lambda s, j, pt, u=u: (layer, pt[s, j * pps + u], 0, 0)) for u in range(pps)]


def _attn_a_s(pt, lam, qs, cache, layer, new, blast, bnew, cfar, pps):
    n_seq, n_pages = pt.shape
    n_steps = n_pages // pps
    c3 = lambda a: pl.BlockSpec(a.shape, lambda s, j, pt: (0, 0, 0))
    grid_spec = pltpu.PrefetchScalarGridSpec(
        num_scalar_prefetch=1,
        grid=(n_seq, n_steps),
        in_specs=[_smem(), pl.BlockSpec((None, 4, 32, 128), lambda s, j, pt: (s, 0, 0, 0))]
        + _kv_page_specs(layer, pps)
        + [pl.BlockSpec((None, 8, 1024), lambda s, j, pt: (s, 0, 0)), c3(blast), c3(bnew), c3(cfar)],
        out_specs=pl.BlockSpec((None, 4, 16, 128), lambda s, j, pt: (s, 0, 0, 0)),
        scratch_shapes=_fa_scratch(A_KVH * 32),
    )
    return pl.pallas_call(
        functools.partial(_attn_a_s_kernel, pps=pps, n_steps=n_steps),
        grid_spec=grid_spec,
        out_shape=jax.ShapeDtypeStruct((n_seq, 4, 16, 128), F32),
        compiler_params=_cp(("parallel", "arbitrary")),
        name="attn_a_s",
    )(pt, lam, qs, *([cache] * (2 * pps)), new, blast, bnew, cfar)


def _attn_b_s_kernel(pt_ref, q_ref, cq_ref, *rest, pps, n_steps):
    kwin, vwin = rest[:pps], rest[pps:2 * pps]
    ck_ref, ckn_ref, new_ref, tri_ref, o_ref, m_ref, l_ref, acc_ref = rest[2 * pps:]
    j = pl.program_id(1)

    @pl.when(j == 0)
    def _():
        _fa_init(m_ref, l_ref, acc_ref)

    def decay(ck, kh, lo):
        rows = [jnp.broadcast_to(ck[kh, g:g + 1, lo:lo + LANE], (8, LANE)) for g in range(2)]
        return cq_ref[kh] - jnp.concatenate(rows, axis=0)

    s, vgroups = [], []
    for kh in range(B_KVH):
        s.append(jnp.concatenate([_dot_t(q_ref[kh], _head_rows(w, kh).astype(BF16)) + decay(ck_ref, kh, u * LANE)
                                  for u, w in enumerate(kwin)], axis=1))
        vgroups.append([_head_rows(w, kh).astype(BF16) for w in vwin])
    _fa_update(jnp.concatenate(s, axis=0), vgroups, m_ref, l_ref, acc_ref)

    @pl.when(j == n_steps - 1)
    def _():
        s, vgroups = [], []
        for kh in range(B_KVH):
            kn = _pad_rows(new_ref[:, kh * 256:kh * 256 + 128], LANE).astype(BF16)
            s.append(_dot_t(q_ref[kh], kn) + decay(ckn_ref, kh, 0) + tri_ref[...])
            vgroups.append([_pad_rows(new_ref[:, kh * 256 + 128:(kh + 1) * 256], LANE).astype(BF16)])
        _fa_update(jnp.concatenate(s, axis=0), vgroups, m_ref, l_ref, acc_ref)
        o = acc_ref[...] / l_ref[...]
        for kh in range(B_KVH):
            o_ref[kh] = o[kh * 16:(kh + 1) * 16]


def _attn_b_s(pt, qs, cqb, cache, layer, cst, new, tri, pps):
    n_seq, n_pages = pt.shape
    n_steps = n_pages // pps
    grid_spec = pltpu.PrefetchScalarGridSpec(
        num_scalar_prefetch=1,
        grid=(n_seq, n_steps),
        in_specs=[pl.BlockSpec((None, 4, 16, 128), lambda s, j, pt: (s, 0, 0, 0)),
                  pl.BlockSpec((None, 4, 16, 128), lambda s, j, pt: (s, 0, 0, 0))]
        + _kv_page_specs(layer, pps)
        + [pl.BlockSpec((None, 4, 2, pps * 128), lambda s, j, pt: (s, 0, 0, j)),
           pl.BlockSpec((None, 4, 2, 128), lambda s, j, pt: (s, 0, 0, n_pages)),
           pl.BlockSpec((None, 8, 1024), lambda s, j, pt: (s, 0, 0)),
           pl.BlockSpec(tri.shape, lambda s, j, pt: (0, 0))],
        out_specs=pl.BlockSpec((None, 4, 16, 128), lambda s, j, pt: (s, 0, 0, 0)),
        scratch_shapes=_fa_scratch(B_KVH * 16),
    )
    return pl.pallas_call(
        functools.partial(_attn_b_s_kernel, pps=pps, n_steps=n_steps),
        grid_spec=grid_spec,
        out_shape=jax.ShapeDtypeStruct((n_seq, 4, 16, 128), F32),
        compiler_params=_cp(("parallel", "arbitrary")),
        name="attn_b_s",
    )(pt, qs, cqb, *([cache] * (2 * pps)), cst, cst, new, tri)


def _nsa_s1_kernel(pt_ref, q_ref, *rest, pps, n_steps, n_cmp, n_sel, nsp, q0):
    pages = rest[:pps]
    pe_ref, w1_ref, w2_ref, kn_ref, bc_ref, ocmp_ref, sel_ref, xk_ref, xv_ref, flat_ref = rest[pps:]
    j = pl.program_id(1)
    for u, w in enumerate(pages):
        r0 = pl.multiple_of((j * pps + u) * PAGE, PAGE)
        for kh in range(C_KVH):
            xk_ref[kh, pl.ds(r0, PAGE), :] = _sub(w, 4 * kh)
            xv_ref[kh, pl.ds(r0, PAGE), :] = _sub(w, 4 * kh + 1)

    @pl.when(j == n_steps - 1)
    def _():
        for kh in range(C_KVH):
            kc, vc = _compress((xk_ref.at[kh], xv_ref.at[kh]), flat_ref, pe_ref, w1_ref, w2_ref, kn_ref, n_cmp)
            q = q_ref[kh]
            qg = [q[g * 8:(g + 1) * 8] for g in range(C_GRP)]
            ocmp, sel = _cmp_and_select(qg, kc.astype(BF16), vc.astype(BF16),
                                        lambda g, kh=kh: bc_ref[kh, g * 8:(g + 1) * 8], q0, 8, nsp, n_sel)
            for g in range(C_GRP):
                ocmp_ref[kh, g * 8:(g + 1) * 8, :] = ocmp[g]
            sel_ref[kh] = sel


def _nsa_s1(pt, qs, cache, layer, pe, w1, w2, kn0, bias_cmp, pps, n_sel, nsp, q0):
    n_seq, n_pages = pt.shape
    n_steps = n_pages // pps
    n_cmp = n_pages * PAGE // L_BLK
    whole = lambda a: pl.BlockSpec(a.shape, lambda s, j, pt: (0,) * a.ndim)
    per_s = lambda r, c: pl.BlockSpec((None, C_KVH, r, c), lambda s, j, pt: (s, 0, 0, 0))
    grid_spec = pltpu.PrefetchScalarGridSpec(
        num_scalar_prefetch=1,
        grid=(n_seq, n_steps),
        in_specs=[per_s(32, 128)] + _page_specs(layer, pps)
        + [whole(pe), whole(w1), whole(w2), whole(kn0), whole(bias_cmp)],
        out_specs=[per_s(32, 128), per_s(8, nsp)],
        scratch_shapes=[pltpu.VMEM((C_KVH, n_pages * PAGE, 128), F32), pltpu.VMEM((C_KVH, n_pages * PAGE, 128), F32),
                        pltpu.VMEM((n_cmp, L_BLK * C_HD), BF16)],
    )
    return pl.pallas_call(
        functools.partial(_nsa_s1_kernel, pps=pps, n_steps=n_steps, n_cmp=n_cmp, n_sel=n_sel, nsp=nsp, q0=q0),
        grid_spec=grid_spec,
        out_shape=[jax.ShapeDtypeStruct((n_seq, C_KVH, 32, 128), F32),
                   jax.ShapeDtypeStruct((n_seq, C_KVH, 8, nsp), F32)],
        compiler_params=_cp(("parallel", "arbitrary")),
        name="nsa_s1",
    )(pt, qs, *([cache] * pps), pe, w1, w2, kn0, bias_cmp)


def _nsa_s2_kernel(pt_ref, q_ref, *rest, pps, n_steps):
    pages = rest[:pps]
    (am_ref, amn_ref, new_ref, wst_ref, wnew_ref, blast_ref, bnew_ref, cfar_ref, bwin_ref, gt_ref, ocmp_ref,
     o_ref, m_ref, l_ref, acc_ref) = rest[pps:]
    j = pl.program_id(1)
    last = j == n_steps - 1

    @pl.when(j == 0)
    def _():
        _fa_init(m_ref, l_ref, acc_ref)

    s, vgroups = [], []
    for kh in range(C_KVH):
        q = q_ref[kh]
        far = cfar_ref[kh]
        row = []
        for u, w in enumerate(pages):
            bias = jnp.where(last, blast_ref[kh], far) if u == pps - 1 else far
            row.append(_dot_t(q, _sub(w, 4 * kh + 2).astype(BF16)) + bias
                       + _rows(am_ref[kh, :, u * LANE:(u + 1) * LANE], C_GRP))
        s.append(jnp.concatenate(row, axis=1))
        vgroups.append([_sub(w, 4 * kh + 3).astype(BF16) for w in pages])
    _fa_update(jnp.concatenate(s, axis=0), vgroups, m_ref, l_ref, acc_ref)

    @pl.when(last)
    def _():
        s, vgroups = [], []
        for kh in range(C_KVH):
            c0 = kh * 512 + 256
            kn = _pad_rows(new_ref[:, c0:c0 + 128], LANE).astype(BF16)
            s.append(_dot_t(q_ref[kh], kn) + bnew_ref[kh] + _rows(amn_ref[kh], C_GRP))
            vgroups.append([_pad_rows(new_ref[:, c0 + 128:c0 + 256], LANE).astype(BF16)])
        _fa_update(jnp.concatenate(s, axis=0), vgroups, m_ref, l_ref, acc_ref)
        o_sel_all = acc_ref[...] / l_ref[...]
        for kh in range(C_KVH):
            q = q_ref[kh]
            o_sel = o_sel_all[kh * 32:(kh + 1) * 32]
            w0 = kh * 256
            wk = wst_ref[:, w0:w0 + 128].astype(BF16)
            wv = wst_ref[:, w0 + 128:w0 + 256].astype(BF16)
            wkn = _pad_rows(wnew_ref[:, w0:w0 + 128], LANE).astype(BF16)
            wvn = _pad_rows(wnew_ref[:, w0 + 128:w0 + 256], LANE).astype(BF16)
            nst = wk.shape[0]
            sw = jnp.concatenate([_dot_t(q, wk) + bwin_ref[kh], _dot_t(q, wkn) + bnew_ref[kh]], axis=1)
            p = jnp.exp(sw - jnp.max(sw, axis=-1, keepdims=True))
            o_win = (jnp.dot(p[:, 0:nst].astype(BF16), wv, preferred_element_type=F32)
                     + jnp.dot(p[:, nst:].astype(BF16), wvn, preferred_element_type=F32)) / jnp.sum(p, axis=-1, keepdims=True)
            gt = gt_ref[kh]
            o_ref[kh] = gt[:, 0:128] * ocmp_ref[kh] + gt[:, 128:256] * o_sel + gt[:, 256:384] * o_win


def _nsa_s2(pt, qs, cache, layer, addmask, c_new8, win_state, w_new8, blast, bnew, cfar, bwin, gates, ocmp, pps):
    n_seq, n_pages = pt.shape
    n_steps = n_pages // pps
    nst = win_state.shape[2]
    whole = lambda a: pl.BlockSpec(a.shape, lambda s, j, pt: (0,) * a.ndim)
    per_s = lambda r, c: pl.BlockSpec((None, C_KVH, r, c), lambda s, j, pt: (s, 0, 0, 0))
    grid_spec = pltpu.PrefetchScalarGridSpec(
        num_scalar_prefetch=1,
        grid=(n_seq, n_steps),
        in_specs=[per_s(32, 128)] + _page_specs(layer, pps)
        + [pl.BlockSpec((None, C_KVH, 8, pps * 128), lambda s, j, pt: (s, 0, 0, j)),
           pl.BlockSpec((None, C_KVH, 8, 128), lambda s, j, pt: (s, 0, 0, n_pages)),
           pl.BlockSpec((None, 8, c_new8.shape[2]), lambda s, j, pt: (s, 0, 0)),
           pl.BlockSpec((None, None, nst, win_state.shape[3]), lambda s, j, pt: (layer, s, 0, 0)),
           pl.BlockSpec((None, 8, w_new8.shape[2]), lambda s, j, pt: (s, 0, 0)),
           whole(blast), whole(bnew), whole(cfar), whole(bwin), per_s(32, 384), per_s(32, 128)],
        out_specs=per_s(32, 128),
        scratch_shapes=_fa_scratch(C_KVH * 32),
    )
    return pl.pallas_call(
        functools.partial(_nsa_s2_kernel, pps=pps, n_steps=n_steps),
        grid_spec=grid_spec,
        out_shape=jax.ShapeDtypeStruct((n_seq, C_KVH, 32, 128), F32),
        compiler_params=_cp(("parallel", "arbitrary")),
        name="nsa_s2",
    )(pt, qs, *([cache] * pps), addmask, addmask, c_new8, win_state, w_new8, blast, bnew, cfar, bwin, gates, ocmp)


def _final_kernel(x_ref, oa_ref, ob_ref, oc_ref, ga_ref, gb_ref, gc_ref, mg_ref, sub_ref, wb_ref, wo_ref, y_ref,
                  *, c_sub, d):
    y = None
    for n, (o_ref, g_ref) in enumerate(((oa_ref, ga_ref), (ob_ref, gb_ref), (oc_ref, gc_ref))):
        parts = []
        for h in range(8):
            sl = slice(h * LANE, (h + 1) * LANE)
            o = o_ref[:, sl].astype(F32)
            if n == 0:
                o = _rms(o, sub_ref[...]) * c_sub
            g = g_ref[:, sl].astype(F32)
            parts.append((o * (g * _sigmoid(g))).astype(BF16))
        proj = jnp.dot(jnp.concatenate(parts, axis=1), wb_ref[n], preferred_element_type=F32)
        term = _sigmoid(mg_ref[:, n * d:(n + 1) * d].astype(F32)) * proj
        y = term if y is None else y + term
    y_ref[...] = x_ref[...] + jnp.dot(y.astype(BF16), wo_ref[...], preferred_element_type=F32)


def _final(x, oa, ob, oc, z, sub, wb, wo, c_sub):
    m, d = x.shape
    tm = min(m, 256)
    row = lambda w, blk=0: pl.BlockSpec((tm, w), lambda i, blk=blk: (i, blk))
    once = lambda a: pl.BlockSpec(a.shape, lambda i: (0,) * a.ndim, pipeline_mode=pl.Buffered(1))
    return pl.pallas_call(
        functools.partial(_final_kernel, c_sub=c_sub, d=d),
        grid=(m // tm,),
        in_specs=[row(d), row(1024), row(1024), row(1024), row(1024, BLK_AG), row(1024, BLK_BG), row(1024, BLK_CG),
                  row(3 * d, 0), once(sub), once(wb), once(wo)],
        out_specs=row(d),
        out_shape=jax.ShapeDtypeStruct((m, d), F32),
        compiler_params=_cp(("parallel",), 56),
        name="final",
    )(x, oa, ob, oc, z, z, z, z, sub, wb, wo)


def _bucket_np(dist):
    n = np.maximum(dist, 0)
    exact = REL_BUCKETS // 2
    nf = np.maximum(n, exact).astype(np.float32)
    big = exact + (np.log(nf / np.float32(exact)) / np.float32(math.log(REL_MAX_DIST / exact))
                   * np.float32(REL_BUCKETS - exact)).astype(np.int32)
    return np.where(n < exact, n, np.minimum(big, REL_BUCKETS - 1)).astype(np.int32)


def _rel_table(rel, dist, valid):
    tab = jnp.take(rel.astype(F32), jnp.asarray(_bucket_np(dist)), axis=0)
    tab = jnp.where(jnp.asarray(valid)[..., None], tab, NEG)
    return jnp.moveaxis(tab, -1, 0)


def _toeplitz(rel, off, n, valid):
    d = off + np.arange(n - 1, -n, -1)
    w = _rel_table(rel, d, valid(d))
    w = jnp.pad(w, ((0, 0), (0, 1)))
    skew = jnp.tile(w, (1, n))[:, :n * (2 * n - 1)].reshape(-1, n, 2 * n - 1)
    return skew[:, :, n - 1:]


def _prompt_tables(rel, tq, with_window):
    tabs = [_toeplitz(rel, 0, tq, lambda d: d >= 0), _toeplitz(rel, tq, tq, lambda d: d >= 0)]
    if with_window:
        tabs.append(_toeplitz(rel, WINDOW, tq, lambda d: d <= WINDOW))
    return jnp.stack(tabs, axis=1)


def _cmp_table(rel, t, n_cmp, ncp):
    d0 = L_BLK * (n_cmp - 1) + L_BLK - 1
    d = np.arange(-d0, t)
    f = _rel_table(rel, d, d >= 0)
    cols = [lax.slice_in_dim(f, d0 - L_BLK * j - (L_BLK - 1), d0 - L_BLK * j - (L_BLK - 1) + t, axis=1)
            for j in range(n_cmp)]
    tab = jnp.stack(cols, axis=2)
    return jnp.pad(tab, ((0, 0), (0, 0), (0, ncp - n_cmp)), constant_values=NEG)


def _layer_weights(l, rms_gain, w_in, b_forget, a_q_norm, a_k_norm, a_lambda, a_subln, b_q_norm, b_k_norm, c_q_norm,
                   c_k_norm, c_cmp_pe, c_cmp_w1, c_cmp_w2, w_branch, w_out):
    w = w_in[l]
    d = w.shape[0]
    w_perm = jnp.concatenate([w[:, a:b] for a, b in _PERM] + [jnp.zeros((d, N_PAD - N_USED), F32)],
                             axis=1).astype(BF16)
    two = lambda g: jnp.concatenate([g, g])
    pn = jnp.stack([two(a_q_norm[l]), two(a_k_norm[l]), b_q_norm[l], b_k_norm[l], c_q_norm[l],
                    c_k_norm[l, 0], c_k_norm[l, 1], c_k_norm[l, 2]]).astype(F32)
    bf = jnp.zeros((1, LANE), F32).at[0, :B_HEADS].set(b_forget[l])
    al = a_lambda[l].astype(F32)
    lam_init = 0.8 - 0.6 * math.exp(-0.3 * l)
    lam = (jnp.exp(jnp.sum(al[0] * al[1])) - jnp.exp(jnp.sum(al[2] * al[3])) + lam_init).reshape(1)
    return dict(g=rms_gain[l], w=w_perm, pn=pn, bf=bf, lam=lam, c_sub=1.0 - lam_init,
                sub=a_subln[l].reshape(1, LANE), pe=c_cmp_pe[l], w1=c_cmp_w1[l].astype(BF16),
                w2=c_cmp_w2[l].astype(BF16), kn0=c_k_norm[l, 0].reshape(1, LANE),
                wb=w_branch[l].astype(BF16), wo=w_out[l].astype(BF16))


def _prompt_layer(x, lw, tb, tq):
    bsz, t, d = x.shape
    m = bsz * t
    zg, zr = _proj(x.reshape(m, d), lw["g"], lw["w"])
    qa, qb, qc, a_new, b_new, c_new, w_new, sm = _post(zr, lw["pn"], lw["bf"])
    r3 = lambda a: a.reshape(bsz, t, a.shape[-1])
    qa, qb, qc, a_new, b_new, c_new, w_new, sm = map(r3, (qa, qb, qc, a_new, b_new, c_new, w_new, sm))
    logf = sm[..., :B_HEADS]
    o_a = _attn_a(qa, a_new, tb["a_tabs"], tb["a_far"], lw["lam"], tq)
    csum = jnp.cumsum(logf, axis=1).reshape(bsz, t, B_KVH, 2)
    o_b = _attn_b(qb, b_new, jnp.transpose(csum, (0, 2, 1, 3)), jnp.transpose(csum, (0, 2, 3, 1)), tb["tri"], tq)
    kc, vc = _cmp_p(c_new, lw["pe"], lw["w1"], lw["w2"], lw["kn0"], tb["ncp"])
    gates = jnp.transpose(sm[..., 8:8 + 3 * C_HEADS].reshape(bsz, t, C_KVH, 3 * C_GRP), (0, 2, 1, 3))
    o_c = _nsa_p(qc, kc, vc, c_new, w_new, gates, tb["c_cmp"], tb["c_tabs"], tb["expand"], tb["c_far"], tq)
    y = _final(x.reshape(m, d), o_a.reshape(m, 1024), o_b.reshape(m, 1024), o_c.reshape(m, 1024), zg,
               lw["sub"], lw["wb"], lw["wo"], lw["c_sub"])
    keep = min(WINDOW, t)
    return (y.reshape(bsz, t, d), a_new.reshape(bsz, t, A_KVH, 256), b_new.reshape(bsz, t, B_KVH, 256), logf,
            c_new.reshape(bsz, t, C_KVH, 4, C_HD), w_new[:, t - keep:].reshape(bsz, keep, C_KVH, 2, C_HD))


def _stack_q(q, n_seq, t, kvh, grp):
    q = q.reshape(n_seq, t, kvh, grp, LANE)
    q = jnp.pad(q, ((0, 0), (0, 8 - t), (0, 0), (0, 0), (0, 0)))
    return jnp.transpose(q, (0, 2, 3, 1, 4)).reshape(n_seq, kvh, grp * 8, LANE)


def _unstack_o(o, n_seq, t, kvh, grp):
    o = o.reshape(n_seq, kvh, grp, 8, LANE)[:, :, :, :t]
    return jnp.transpose(o, (0, 3, 1, 2, 4)).reshape(n_seq * t, kvh * grp * LANE)


def _sample_layer(x, l, lw, tb, caches, page_table, pps):
    n_seq, t, d = x.shape
    m = n_seq * t
    cache_a, cache_b, cache_lf, cache_c, win_state = caches
    n_pages = page_table.shape[1]
    past = n_pages * PAGE
    zg, zr = _proj(x.reshape(m, d), lw["g"], lw["w"])
    qa, qb, qc, a_new, b_new, c_new, w_new, sm = _post(zr, lw["pn"], lw["bf"])
    logf = sm[:, :B_HEADS].reshape(n_seq, t, B_HEADS)
    pad8 = lambda a: jnp.pad(a.reshape(n_seq, t, a.shape[-1]), ((0, 0), (0, 8 - t), (0, 0)))
    a_new8, b_new8, c_new8, w_new8 = map(pad8, (a_new, b_new, c_new, w_new))
    qa5 = qa.reshape(n_seq, t, A_KVH, 2, 2, A_HD)
    zq = jnp.zeros_like(qa5[..., 0, :])
    qa_m = jnp.stack([jnp.concatenate([qa5[..., 0, :], zq], -1), jnp.concatenate([zq, qa5[..., 1, :]], -1)], axis=4)
    qs_a = _stack_q(qa_m.reshape(m, A_KVH * 4 * LANE), n_seq, t, A_KVH, 4)
    o_a = _attn_a_s(page_table, lw["lam"], qs_a, cache_a, l, a_new8, tb["a_last"], tb["a_new"], tb["a_far"], pps)
    o_a = _unstack_o(o_a, n_seq, t, A_KVH, 2)
    lf_past = cache_lf[l][page_table].reshape(n_seq, past, B_HEADS).astype(F32)
    csum = jnp.cumsum(jnp.concatenate([lf_past, logf], axis=1), axis=1)
    cst = jnp.pad(jnp.transpose(csum, (0, 2, 1)), ((0, 0), (0, 0), (0, 128 - t))).reshape(n_seq, B_KVH, 2, past + 128)
    cq = jnp.pad(csum[:, past:], ((0, 0), (0, 8 - t), (0, 0))).reshape(n_seq, 8, B_KVH, 2)
    cqb = jnp.broadcast_to(jnp.transpose(cq, (0, 2, 3, 1)).reshape(n_seq, B_KVH, 16, 1), (n_seq, B_KVH, 16, LANE))
    o_b = _attn_b_s(page_table, _stack_q(qb, n_seq, t, B_KVH, 2), cqb, cache_b, l, cst, b_new8, tb["tri"], pps)
    o_b = _unstack_o(o_b, n_seq, t, B_KVH, 2)
    qs_c = _stack_q(qc, n_seq, t, C_KVH, C_GRP)
    n_sel = -(-(past + t) // L_BLK)
    ocmp, sel = _nsa_s1(page_table, qs_c, cache_c, l, lw["pe"], lw["w1"], lw["w2"], lw["kn0"], tb["c_cmp"], PPS_CMP,
                        n_sel, tb["nsp"], past)
    selx = jnp.repeat(sel[..., :n_sel], L_BLK, axis=-1)
    addmask = (jnp.pad(selx, ((0, 0),) * 3 + ((0, past + 128 - n_sel * L_BLK),)) - 1.0) * (-NEG)
    g = jnp.pad(sm[:, 8:8 + 3 * C_HEADS].reshape(n_seq, t, C_KVH, C_GRP, 3), ((0, 0), (0, 8 - t)) + ((0, 0),) * 3)
    g = jnp.transpose(g, (0, 2, 4, 3, 1)).reshape(n_seq, C_KVH, 3, 32, 1)
    gates = jnp.transpose(jnp.broadcast_to(g, (n_seq, C_KVH, 3, 32, LANE)), (0, 1, 3, 2, 4)).reshape(n_seq, C_KVH, 32, 384)
    o_c = _nsa_s2(page_table, qs_c, cache_c, l, addmask, c_new8, win_state, w_new8, tb["c_last"], tb["c_new"],
                  tb["c_far"], tb["c_win"], gates, ocmp, pps)
    o_c = _unstack_o(o_c, n_seq, t, C_KVH, C_GRP)
    y = _final(x.reshape(m, d), o_a, o_b, o_c, zg, lw["sub"], lw["wb"], lw["wo"], lw["c_sub"])
    w_all = jnp.concatenate([win_state[l].reshape(n_seq, -1, C_KVH, 2, C_HD), w_new.reshape(n_seq, t, C_KVH, 2, C_HD)], 1)
    keep = win_state.shape[2]
    return (y.reshape(n_seq, t, d), a_new.reshape(n_seq, t, A_KVH, 256), b_new.reshape(n_seq, t, B_KVH, 256), logf,
            c_new.reshape(n_seq, t, C_KVH, 4, C_HD), w_all[:, w_all.shape[1] - keep:])


def _sample_tables(rel_a, rel_c, t, past, n_cmp, nst):
    tok = np.arange(8)
    jj = np.arange(LANE)
    d_last = LANE + tok[:, None] - jj[None, :]
    d_new = tok[:, None] - jj[None, :]
    v_new = (d_new >= 0) & (jj[None, :] < t)
    ones = np.ones((8, LANE), bool)

    def rows(rel, dist, valid, kvh, per):
        tab = _rel_table(rel, dist, valid)
        return tab.reshape(kvh, per * 8, dist.shape[1])

    tb = dict(a_last=rows(rel_a, d_last, ones, A_KVH, 4), a_new=rows(rel_a, d_new, v_new, A_KVH, 4),
              c_last=rows(rel_c, d_last, ones, C_KVH, C_GRP), c_new=rows(rel_c, d_new, v_new, C_KVH, C_GRP))
    far = lambda rel, kvh, per: jnp.broadcast_to(
        jnp.repeat(rel[REL_BUCKETS - 1].astype(F32), 8).reshape(kvh, per * 8, 1), (kvh, per * 8, LANE))
    tb["a_far"] = far(rel_a, A_KVH, 4)
    tb["c_far"] = far(rel_c, C_KVH, C_GRP)
    ws = np.arange(nst)
    d_win = nst + tok[:, None] - ws[None, :]
    tb["c_win"] = rows(rel_c, d_win, (d_win >= 0) & (d_win <= WINDOW), C_KVH, C_GRP)
    d_cmp = past + tok[:, None] - (L_BLK * np.arange(n_cmp)[None, :] + L_BLK - 1)
    tb["c_cmp"] = rows(rel_c, d_cmp, d_cmp >= 0, C_KVH, C_GRP)
    tri = np.where(v_new, 0.0, NEG).astype(np.float32)
    tb["tri"] = jnp.asarray(np.concatenate([tri, tri], axis=0))
    return tb


def kernel(x_prompt, x_sample, cache_a_kv, cache_b_kv, cache_b_logf, cache_c_kv, state_c_win, page_table, rms_gain,
           w_in, b_forget, a_q_norm, a_k_norm, a_lambda, a_subln, b_q_norm, b_k_norm, c_q_norm, c_k_norm, c_cmp_pe,
           c_cmp_w1, c_cmp_w2, rel_bias, w_branch, w_out):
    depth = w_in.shape[0]
    bsz, t, d = x_prompt.shape
    n_seq, t_s, _ = x_sample.shape
    n_pages = page_table.shape[1]
    past = n_pages * cache_a_kv.shape[2]
    nst = state_c_win.shape[2]
    assert cache_a_kv.shape[2] == PAGE and past % L_BLK == 0 and t_s <= 8 and nst == WINDOW and past >= WINDOW
    tq = min(256, t)
    assert t % tq == 0 and WINDOW % tq == 0 and WINDOW // tq >= 2 and tq % L_BLK == 0
    pps = PPS_ATTN
    assert n_pages % PPS_ATTN == 0 and n_pages % PPS_CMP == 0

    rel_a = rel_bias[:, :2 * A_HEADS]
    rel_c = rel_bias[:, 2 * A_HEADS:]
    n_cmp_p = t // L_BLK
    ncp = -(-n_cmp_p // LANE) * LANE
    nsp_p = -(-(-(-t // L_BLK)) // LANE) * LANE
    tri = np.where(np.arange(tq)[:, None] >= np.arange(tq)[None, :], 0.0, NEG).astype(np.float32)
    expand = (np.arange(t)[None, :] // L_BLK == np.arange(nsp_p)[:, None]).astype(np.float32)
    tb_p = dict(
        a_tabs=_prompt_tables(rel_a, tq, False).reshape(A_KVH, 4, 2, tq, tq).transpose(0, 2, 1, 3, 4)
        .reshape(A_KVH, 2, 4 * tq, tq),
        a_far=rel_a[REL_BUCKETS - 1].astype(F32),
        c_tabs=_prompt_tables(rel_c, tq, True).reshape(C_KVH, C_GRP, 3, tq, tq).transpose(0, 2, 1, 3, 4)
        .reshape(C_KVH, 3, C_GRP * tq, tq),
        c_far=rel_c[REL_BUCKETS - 1].astype(F32),
        c_cmp=_cmp_table(rel_c, t, n_cmp_p, ncp),
        tri=jnp.asarray(np.concatenate([tri, tri], axis=0)), expand=jnp.asarray(expand, dtype=BF16), ncp=ncp)
    n_sel_s = -(-(past + t_s) // L_BLK)
    tb_s = _sample_tables(rel_a, rel_c, t_s, past, past // L_BLK, nst)
    tb_s["nsp"] = -(-n_sel_s // LANE) * LANE

    caches = (cache_a_kv, cache_b_kv, cache_b_logf,
              cache_c_kv.reshape(depth, cache_c_kv.shape[1], PAGE * SUBROWS, LANE),
              state_c_win.reshape(depth, n_seq, nst, C_KVH * 2 * C_HD))
    yp, ys = x_prompt, x_sample
    st_p, st_s = [], []
    for l in range(depth):
        lw = _layer_weights(l, rms_gain, w_in, b_forget, a_q_norm, a_k_norm, a_lambda, a_subln, b_q_norm, b_k_norm,
                            c_q_norm, c_k_norm, c_cmp_pe, c_cmp_w1, c_cmp_w2, w_branch, w_out)
        yp, *sp = _prompt_layer(yp, lw, tb_p, tq)
        ys, *ss = _sample_layer(ys, l, lw, tb_s, caches, page_table, pps)
        st_p.append(sp)
        st_s.append(ss)
    stk = lambda lst, i: jnp.stack([s[i] for s in lst])
    return (yp, ys, stk(st_p, 0), stk(st_s, 0), stk(st_p, 1), stk(st_s, 1), stk(st_p, 2), stk(st_s, 2),
            stk(st_p, 3), stk(st_s, 3), stk(st_p, 4), stk(st_s, 4))
```

```python
import functools
import math

import numpy as np
import jax
import jax.numpy as jnp
from jax import lax
from jax.experimental import pallas as pl
from jax.experimental.pallas import tpu as pltpu

F32 = jnp.float32
BF16 = jnp.bfloat16

A_HD, A_VD, A_HEADS, A_KVH = 64, 128, 8, 4
B_HD, B_HEADS, B_KVH = 128, 8, 4
C_HD, C_HEADS, C_KVH, C_GRP = 128, 8, 2, 4
L_BLK, N_SEL, WINDOW = 64, 16, 512
REL_BUCKETS, REL_MAX_DIST = 32, 128
EPS, NEG, FORCE = 1e-6, -1e30, 1e6
SCALE_A, SCALE_BC = A_HD ** -0.5, 128 ** -0.5
LANE = 128
PAGE = 128
VMEM_MB = 48

_PERM = [(9760, 15904), (2048, 3072), (5128, 6152), (8736, 9760),
         (0, 1024), (3072, 4096), (6152, 7176), (1024, 1536), (1536, 2048), (4096, 4608), (4608, 5120),
         (7176, 8712), (5120, 5128), (8712, 8736)]
N_USED = 15904
TN = 768
N_GATE = 9216
N_REST = 6912
N_PAD = N_GATE + N_REST
BLK_AG, BLK_BG, BLK_CG = 6, 7, 8
BLK_AQ, BLK_BQ, BLK_CQ = 0, 1, 2
BLK_AK, BLK_AV, BLK_BK, BLK_BV, BLK_C0, BLK_C1, BLK_C2 = 6, 7, 8, 9, 10, 11, 12
BLK_SM = 52


def _cp(sem, mb=VMEM_MB):
    return pltpu.CompilerParams(dimension_semantics=sem, vmem_limit_bytes=mb * 1024 * 1024)


def _smem():
    return pl.BlockSpec(memory_space=pltpu.SMEM)


def _rms(x, g):
    ms = jnp.mean(x * x, axis=-1, keepdims=True)
    return x * lax.rsqrt(ms + EPS) * g


def _rms_half(x, g2):
    sq = x * x
    lo = lax.broadcasted_iota(jnp.int32, x.shape, 1) < 64
    s_lo = jnp.sum(jnp.where(lo, sq, 0.0), axis=-1, keepdims=True)
    s_hi = jnp.sum(jnp.where(lo, 0.0, sq), axis=-1, keepdims=True)
    ms = jnp.where(lo, s_lo, s_hi) * (1.0 / 64)
    return x * lax.rsqrt(ms + EPS) * g2


def _sigmoid(x):
    return 1.0 / (1.0 + jnp.exp(-x))


def _dot_t(a, b):
    return lax.dot_general(a, b, (((1,), (1,)), ((), ())), preferred_element_type=F32)


def _lanes(x, n):
    return x if n == LANE else jnp.concatenate([x] * (n // LANE), axis=1)


def _rows(x, times):
    return x if times == 1 else jnp.concatenate([x] * times, axis=0)


def _fa_update(s, vgroups, m_ref, l_ref, acc_ref, shift=None):
    m_prev = m_ref[...]
    rmax = jnp.max(s, axis=-1, keepdims=True)
    if shift is not None:
        rmax = rmax + shift
    m_new = jnp.maximum(m_prev, rmax)
    alpha = jnp.exp(m_prev - m_new)
    off = m_new if shift is None else m_new - shift
    p = jnp.exp(s - _lanes(off, s.shape[1]))
    l_ref[...] = alpha * l_ref[...] + jnp.sum(p, axis=-1, keepdims=True)
    pb = p.astype(BF16)
    rg = s.shape[0] // len(vgroups)
    pv = []
    for gi, vs in enumerate(vgroups):
        lo, part = 0, None
        for v in vs:
            d = jnp.dot(pb[gi * rg:(gi + 1) * rg, lo:lo + v.shape[0]], v, preferred_element_type=F32)
            part = d if part is None else part + d
            lo += v.shape[0]
        pv.append(part)
    acc_ref[...] = alpha * acc_ref[...] + (pv[0] if len(pv) == 1 else jnp.concatenate(pv, axis=0))
    m_ref[...] = m_new


def _fa_init(m_ref, l_ref, acc_ref):
    m_ref[...] = jnp.full(m_ref.shape, NEG, F32)
    l_ref[...] = jnp.zeros(l_ref.shape, F32)
    acc_ref[...] = jnp.zeros(acc_ref.shape, F32)


def _fa_scratch(rows):
    return [pltpu.VMEM((rows, LANE), F32)] * 3


def _proj_kernel(x_ref, g_ref, w_ref, zg_ref, zr_ref, h_ref, *, gate_tiles):
    j = pl.program_id(1)

    @pl.when(j == 0)
    def _():
        h_ref[...] = _rms(x_ref[...], g_ref[...]).astype(BF16)

    acc = jnp.dot(h_ref[...], w_ref[...], preferred_element_type=F32)

    @pl.when(j < gate_tiles)
    def _():
        zg_ref[...] = acc.astype(BF16)

    @pl.when(j >= gate_tiles)
    def _():
        zr_ref[...] = acc


def _proj(x, g, w):
    m, d = x.shape
    tm = next(c for c in (1024, 512, 256, 128, m) if m % c == 0)
    gt = N_GATE // TN
    return pl.pallas_call(
        functools.partial(_proj_kernel, gate_tiles=gt),
        grid=(m // tm, N_PAD // TN),
        in_specs=[pl.BlockSpec((tm, d), lambda i, j: (i, 0)),
                  pl.BlockSpec((1, d), lambda i, j: (0, 0)),
                  pl.BlockSpec((d, TN), lambda i, j: (0, j))],
        out_specs=[pl.BlockSpec((tm, TN), lambda i, j: (i, jnp.minimum(j, gt - 1))),
                   pl.BlockSpec((tm, TN), lambda i, j: (i, jnp.maximum(j - gt, 0)))],
        out_shape=[jax.ShapeDtypeStruct((m, N_GATE), BF16), jax.ShapeDtypeStruct((m, N_REST), F32)],
        scratch_shapes=[pltpu.VMEM((tm, d), BF16)],
        compiler_params=_cp(("parallel", "arbitrary")),
        name="proj",
    )(x, g.reshape(1, d), w)


def _post_kernel(zaq, zbq, zcq, zak, zav, zbk, zbv, zc0, zc1, zc2, zsm, pn, bf,
                 qa, qb, qc, anew, bnew, cnew, wnew, sm):
    for h in range(8):
        sl = slice(h * LANE, (h + 1) * LANE)
        qa[:, sl] = (_rms_half(zaq[:, sl], pn[0:1]) * SCALE_A).astype(BF16)
        qb[:, sl] = (_rms(zbq[:, sl], pn[2:3]) * SCALE_BC).astype(BF16)
        qc[:, sl] = (_rms(zcq[:, sl], pn[4:5]) * SCALE_BC).astype(BF16)
    for k in range(4):
        src = slice(k * LANE, (k + 1) * LANE)
        anew[:, k * 256:k * 256 + 128] = _rms_half(zak[:, src], pn[1:2])
        anew[:, k * 256 + 128:(k + 1) * 256] = zav[:, src]
        bnew[:, k * 256:k * 256 + 128] = _rms(zbk[:, src], pn[3:4])
        bnew[:, k * 256 + 128:(k + 1) * 256] = zbv[:, src]
    for k in range(2):
        s0 = slice(k * LANE, (k + 1) * LANE)
        s1 = slice(256 + k * LANE, 256 + (k + 1) * LANE)
        cnew[:, k * 512:k * 512 + 128] = zc0[:, s0]
        cnew[:, k * 512 + 128:k * 512 + 256] = zc0[:, s1]
        cnew[:, k * 512 + 256:k * 512 + 384] = _rms(zc1[:, s0], pn[6:7])
        cnew[:, k * 512 + 384:k * 512 + 512] = zc1[:, s1]
        wnew[:, k * 256:k * 256 + 128] = _rms(zc2[:, s0], pn[7:8])
        wnew[:, k * 256 + 128:k * 256 + 256] = zc2[:, s1]
    z = zsm[...]
    x = z + bf[...]
    logf = jnp.minimum(x, 0.0) - jnp.log1p(jnp.exp(-jnp.abs(x)))
    lane = lax.broadcasted_iota(jnp.int32, z.shape, 1)
    sm[...] = jnp.where(lane < 8, logf, _sigmoid(z))


def _post(z, pn, bf):
    m = z.shape[0]
    tm = min(m, 256)

    def zs(width, blk):
        return pl.BlockSpec((tm, width), lambda i, blk=blk: (i, blk))

    def os(width):
        return pl.BlockSpec((tm, width), lambda i: (i, 0))

    const = lambda a: pl.BlockSpec(a.shape, lambda i: (0, 0))
    return pl.pallas_call(
        _post_kernel,
        grid=(m // tm,),
        in_specs=[zs(1024, BLK_AQ), zs(1024, BLK_BQ), zs(1024, BLK_CQ), zs(512, BLK_AK), zs(512, BLK_AV),
                  zs(512, BLK_BK), zs(512, BLK_BV), zs(512, BLK_C0), zs(512, BLK_C1), zs(512, BLK_C2),
                  zs(128, BLK_SM), const(pn), const(bf)],
        out_specs=[os(1024), os(1024), os(1024), os(1024), os(1024), os(1024), os(512), os(128)],
        out_shape=[jax.ShapeDtypeStruct((m, 1024), BF16)] * 3
        + [jax.ShapeDtypeStruct((m, 1024), F32)] * 3
        + [jax.ShapeDtypeStruct((m, 512), F32), jax.ShapeDtypeStruct((m, 128), F32)],
        compiler_params=_cp(("parallel",)),
        name="post",
    )(*([z] * 11), pn, bf)


def _attn_a_kernel(cfar_ref, lam_ref, q_ref, kv_ref, tab_ref, o_ref, m_ref, l_ref, acc_ref, cvec_ref, *, tq, tk):
    kh = pl.program_id(1)
    i = pl.program_id(2)
    q = q_ref[...]
    lo = lax.broadcasted_iota(jnp.int32, (tq, LANE), 1) < 64
    zero = jnp.zeros((tq, LANE), BF16)
    blocks = []
    for g in range(2):
        qh = q[:, g * LANE:(g + 1) * LANE]
        blocks.append(jnp.where(lo, qh, zero))
        blocks.append(jnp.where(lo, zero, qh))
    qs = jnp.concatenate(blocks, axis=0)
    for b in range(4):
        cvec_ref[b * tq:(b + 1) * tq, :] = jnp.full((tq, LANE), cfar_ref[kh * 4 + b], F32)
    _fa_init(m_ref, l_ref, acc_ref)

    def tile(kt, bias, shift):
        r0 = pl.multiple_of(kt * tk, tk)
        k = kv_ref[pl.ds(r0, tk), 0:128].astype(BF16)
        v = kv_ref[pl.ds(r0, tk), 128:256].astype(BF16)
        s = _dot_t(qs, k)
        if bias is not None:
            s = s + bias
        _fa_update(s, [[v]], m_ref, l_ref, acc_ref, shift)

    def far(kt, c):
        tile(kt, None, cvec_ref[...])
        return c

    lax.fori_loop(0, jnp.maximum(i - 1, 0), far, 0)

    @pl.when(i >= 1)
    def _():
        tile(i - 1, tab_ref[1], None)

    tile(i, tab_ref[0], None)
    lam = lam_ref[0]
    o = acc_ref[...] / l_ref[...]
    for g in range(2):
        o_ref[:, g * LANE:(g + 1) * LANE] = (o[2 * g * tq:(2 * g + 1) * tq]
                                             - lam * o[(2 * g + 1) * tq:(2 * g + 2) * tq]).astype(o_ref.dtype)


def _attn_a(qa, a_new, tabs, cfar, lam, tq):
    bsz, t, _ = qa.shape
    return pl.pallas_call(
        functools.partial(_attn_a_kernel, tq=tq, tk=tq),
        grid=(bsz, A_KVH, t // tq),
        in_specs=[_smem(), _smem(),
                  pl.BlockSpec((None, tq, 256), lambda b, k, i: (b, i, k)),
                  pl.BlockSpec((None, t, 256), lambda b, k, i: (b, 0, k)),
                  pl.BlockSpec((None, 2, 4 * tq, tq), lambda b, k, i: (k, 0, 0, 0))],
        out_specs=pl.BlockSpec((None, tq, 256), lambda b, k, i: (b, i, k)),
        out_shape=jax.ShapeDtypeStruct((bsz, t, 1024), BF16),
        scratch_shapes=_fa_scratch(4 * tq) + [pltpu.VMEM((4 * tq, LANE), F32)],
        compiler_params=_cp(("parallel", "parallel", "arbitrary")),
        name="attn_a",
    )(cfar, lam, qa, a_new, tabs)


def _attn_b_kernel(q_ref, kv_ref, cq_ref, ck_ref, tri_ref, o_ref, m_ref, l_ref, acc_ref, cqb_ref, *, tq, tk):
    i = pl.program_id(2)
    q = q_ref[...]
    qs = jnp.concatenate([q[:, 0:128], q[:, 128:256]], axis=0)
    for g in range(2):
        cqb_ref[g * tq:(g + 1) * tq, :] = jnp.broadcast_to(cq_ref[:, g:g + 1], (tq, LANE))
    _fa_init(m_ref, l_ref, acc_ref)

    def tile(kt, diag):
        r0 = pl.multiple_of(kt * tk, tk)
        k = kv_ref[pl.ds(r0, tk), 0:128].astype(BF16)
        v = kv_ref[pl.ds(r0, tk), 128:256].astype(BF16)
        ck = jnp.concatenate([jnp.broadcast_to(ck_ref[g:g + 1, pl.ds(r0, tk)], (tq, tk)) for g in range(2)], axis=0)
        s = _dot_t(qs, k) - ck
        if diag:
            s = s + tri_ref[...]
        _fa_update(s, [[v]], m_ref, l_ref, acc_ref, cqb_ref[...])

    def far(kt, c):
        tile(kt, False)
        return c

    lax.fori_loop(0, i, far, 0)
    tile(i, True)
    o = acc_ref[...] / l_ref[...]
    for g in range(2):
        o_ref[:, g * LANE:(g + 1) * LANE] = o[g * tq:(g + 1) * tq].astype(o_ref.dtype)


def _attn_b(qb, b_new, csq, cst, tri, tq):
    bsz, t, _ = qb.shape
    return pl.pallas_call(
        functools.partial(_attn_b_kernel, tq=tq, tk=tq),
        grid=(bsz, B_KVH, t // tq),
        in_specs=[pl.BlockSpec((None, tq, 256), lambda b, k, i: (b, i, k)),
                  pl.BlockSpec((None, t, 256), lambda b, k, i: (b, 0, k)),
                  pl.BlockSpec((None, None, tq, 2), lambda b, k, i: (b, k, i, 0)),
                  pl.BlockSpec((None, None, 2, t), lambda b, k, i: (b, k, 0, 0)),
                  pl.BlockSpec((2 * tq, tq), lambda b, k, i: (0, 0))],
        out_specs=pl.BlockSpec((None, tq, 256), lambda b, k, i: (b, i, k)),
        out_shape=jax.ShapeDtypeStruct((bsz, t, 1024), BF16),
        scratch_shapes=_fa_scratch(2 * tq) + [pltpu.VMEM((2 * tq, LANE), F32)],
        compiler_params=_cp(("parallel", "parallel", "arbitrary")),
        name="attn_b",
    )(qb, b_new, csq, cst, tri)


def _compress(x_refs, flat_ref, pe_ref, w1_ref, w2_ref, kn_ref, n_cmp):
    outs = []
    for c in range(2):
        for l in range(L_BLK):
            rows = x_refs[c][pl.ds(l, n_cmp, stride=L_BLK), :]
            flat_ref[:, l * LANE:(l + 1) * LANE] = (rows + pe_ref[c, l:l + 1, :]).astype(BF16)
        h = jnp.dot(flat_ref[...], w1_ref[c], preferred_element_type=F32)
        h = h * _sigmoid(h)
        o = jnp.dot(h.astype(BF16), w2_ref[c], preferred_element_type=F32)
        if c == 0:
            o = _rms(o, kn_ref[...])
        outs.append(o)
    return outs


def _cmp_and_select(qg, kc, vc, bias_of, q0, rows, nsp, n_sel):
    ncp = kc.shape[0]
    imp = jnp.zeros((rows, ncp), F32)
    ocmp = []
    for g, q in enumerate(qg):
        bias = bias_of(g)
        s = _dot_t(q, kc) + bias
        e = jnp.exp(s - jnp.max(s, axis=-1, keepdims=True))
        pc = jnp.where(bias > 0.5 * NEG, e / jnp.sum(e, axis=-1, keepdims=True), 0.0)
        imp = imp + pc
        ocmp.append(jnp.dot(pc.astype(BF16), vc, preferred_element_type=F32))
    if nsp > ncp:
        imp = jnp.concatenate([imp, jnp.zeros((rows, nsp - ncp), F32)], axis=1)
    j = lax.broadcasted_iota(jnp.int32, (rows, nsp), 1)
    pos = q0 + lax.broadcasted_iota(jnp.int32, (rows, nsp), 0)
    cur = lax.shift_right_logical(pos, 6)
    imp = imp + FORCE * (jnp.where(j == cur, 4.0, 0.0) + jnp.where(j == cur - 1, 2.0, 0.0)
                         + jnp.where(j == 0, 1.0, 0.0))
    imp = jnp.where(j <= cur, imp, NEG)
    cnt = jnp.zeros((rows, nsp), F32)
    for c in range(n_sel):
        col = imp[:, c:c + 1]
        first = jnp.where(j > c, 1.0, 0.0)
        cnt = cnt + jnp.where(col > imp, 1.0, jnp.where(col == imp, first, 0.0))
    sel = jnp.where(cnt < float(min(N_SEL, n_sel)), 1.0, 0.0)
    return ocmp, sel


def _cmp_p_kernel(xk_ref, xv_ref, pe_ref, w1_ref, w2_ref, kn_ref, kc_ref, vc_ref, flat_ref, *, n_cmp):
    kc, vc = _compress((xk_ref, xv_ref), flat_ref, pe_ref, w1_ref, w2_ref, kn_ref, n_cmp)
    kc_ref[...] = jnp.zeros(kc_ref.shape, BF16)
    vc_ref[...] = jnp.zeros(vc_ref.shape, BF16)
    kc_ref[0:n_cmp, :] = kc.astype(BF16)
    vc_ref[0:n_cmp, :] = vc.astype(BF16)


def _cmp_p(c_new, pe, w1, w2, kn0, ncp):
    bsz, t, _ = c_new.shape
    n_cmp = t // L_BLK
    whole = lambda a: pl.BlockSpec(a.shape, lambda b, k: (0,) * a.ndim)
    return pl.pallas_call(
        functools.partial(_cmp_p_kernel, n_cmp=n_cmp),
        grid=(bsz, C_KVH),
        in_specs=[pl.BlockSpec((None, t, 128), lambda b, k: (b, 0, 4 * k)),
                  pl.BlockSpec((None, t, 128), lambda b, k: (b, 0, 4 * k + 1)),
                  whole(pe), whole(w1), whole(w2), whole(kn0)],
        out_specs=[pl.BlockSpec((None, None, ncp, 128), lambda b, k: (b, k, 0, 0))] * 2,
        out_shape=[jax.ShapeDtypeStruct((bsz, C_KVH, ncp, 128), BF16)] * 2,
        scratch_shapes=[pltpu.VMEM((n_cmp, L_BLK * C_HD), BF16)],
        compiler_params=_cp(("parallel", "parallel")),
        name="nsa_cmp_p",
    )(c_new, c_new, pe, w1, w2, kn0)


def _nsa_p_kernel(cfar_ref, q_ref, kc_ref, vc_ref, sel_ref, win_ref, gt_ref, bc_ref, tab_ref, e_ref, o_ref,
                  ms_ref, ls_ref, accs_ref, mw_ref, lw_ref, accw_ref, cvec_ref, am_ref, *, tq, tk, n_sel, nsp, nw):
    kh = pl.program_id(1)
    i = pl.program_id(2)
    q = q_ref[...]
    qg = [q[:, g * LANE:(g + 1) * LANE] for g in range(C_GRP)]
    qs = jnp.concatenate(qg, axis=0)
    ocmp, sel = _cmp_and_select(qg, kc_ref[...], vc_ref[...], lambda g: bc_ref[g], i * tq, tq, nsp, n_sel)
    am_ref[...] = (jnp.dot(sel.astype(BF16), e_ref[...], preferred_element_type=F32) - 1.0) * (-NEG)
    for g in range(C_GRP):
        cvec_ref[g * tq:(g + 1) * tq, :] = jnp.full((tq, LANE), cfar_ref[kh * C_GRP + g], F32)
    _fa_init(ms_ref, ls_ref, accs_ref)
    _fa_init(mw_ref, lw_ref, accw_ref)

    def tile(src_ref, kt, bias, shift, masked, state):
        r0 = pl.multiple_of(kt * tk, tk)
        k = src_ref[pl.ds(r0, tk), 0:128].astype(BF16)
        v = src_ref[pl.ds(r0, tk), 128:256].astype(BF16)
        s = _dot_t(qs, k)
        if bias is not None:
            s = s + bias
        if masked:
            s = s + _rows(am_ref[:, pl.ds(r0, tk)], C_GRP)
        _fa_update(s, [[v]], *state, shift)

    sel_state = (ms_ref, ls_ref, accs_ref)
    win_state = (mw_ref, lw_ref, accw_ref)

    def far(kt, c):
        tile(sel_ref, kt, None, cvec_ref[...], True, sel_state)
        return c

    lax.fori_loop(0, jnp.maximum(i - 1, 0), far, 0)

    @pl.when(i >= 1)
    def _():
        tile(sel_ref, i - 1, tab_ref[1], None, True, sel_state)
        tile(win_ref, i - 1, tab_ref[1], None, False, win_state)

    tile(sel_ref, i, tab_ref[0], None, True, sel_state)
    tile(win_ref, i, tab_ref[0], None, False, win_state)
    for d in range(2, nw):
        @pl.when(i >= d)
        def _(d=d):
            tile(win_ref, i - d, None, cvec_ref[...], False, win_state)

    @pl.when(i >= nw)
    def _():
        tile(win_ref, i - nw, tab_ref[2], None, False, win_state)

    gt = gt_ref[...]
    o_sel = accs_ref[...] / ls_ref[...]
    o_win = accw_ref[...] / lw_ref[...]
    for g in range(C_GRP):
        rs = slice(g * tq, (g + 1) * tq)
        o_ref[:, g * LANE:(g + 1) * LANE] = (gt[:, 3 * g:3 * g + 1] * ocmp[g] + gt[:, 3 * g + 1:3 * g + 2] * o_sel[rs]
                                            + gt[:, 3 * g + 2:3 * g + 3] * o_win[rs]).astype(o_ref.dtype)


def _nsa_p(qc, kc, vc, c_new, w_new, gates, bias_cmp, tabs, expand, cfar, tq):
    bsz, t, _ = qc.shape
    ncp = kc.shape[2]
    n_sel = -(-t // L_BLK)
    nsp = expand.shape[0]
    kern = functools.partial(_nsa_p_kernel, tq=tq, tk=tq, n_sel=n_sel, nsp=nsp, nw=WINDOW // tq)
    rows = C_GRP * tq
    return pl.pallas_call(
        kern,
        grid=(bsz, C_KVH, t // tq),
        in_specs=[_smem(),
                  pl.BlockSpec((None, tq, 512), lambda b, k, i: (b, i, k)),
                  pl.BlockSpec((None, None, ncp, 128), lambda b, k, i: (b, k, 0, 0)),
                  pl.BlockSpec((None, None, ncp, 128), lambda b, k, i: (b, k, 0, 0)),
                  pl.BlockSpec((None, t, 256), lambda b, k, i: (b, 0, 2 * k + 1)),
                  pl.BlockSpec((None, t, 256), lambda b, k, i: (b, 0, k)),
                  pl.BlockSpec((None, None, tq, 12), lambda b, k, i: (b, k, i, 0)),
                  pl.BlockSpec((C_GRP, tq, ncp), lambda b, k, i: (k, i, 0)),
                  pl.BlockSpec((None, 3, rows, tq), lambda b, k, i: (k, 0, 0, 0)),
                  pl.BlockSpec(expand.shape, lambda b, k, i: (0, 0))],
        out_specs=pl.BlockSpec((None, tq, 512), lambda b, k, i: (b, i, k)),
        out_shape=jax.ShapeDtypeStruct((bsz, t, 1024), BF16),
        scratch_shapes=_fa_scratch(rows) + _fa_scratch(rows) + [pltpu.VMEM((rows, LANE), F32), pltpu.VMEM((tq, t), F32)],
        compiler_params=_cp(("parallel", "parallel", "arbitrary")),
        name="nsa_p",
    )(cfar, qc, kc, vc, c_new, w_new, gates, bias_cmp, tabs, expand)


SUBROWS = 8
PPS_ATTN = 16
PPS_CMP = 8


def _pad_rows(x, n):
    return jnp.concatenate([x, jnp.zeros((n - x.shape[0], x.shape[1]), x.dtype)], axis=0)


def _sub(page_ref, s):
    return page_ref[pl.ds(s, PAGE, stride=SUBROWS), :]


def _head_rows(win_ref, kh):
    return win_ref.reshape(4 * PAGE, LANE)[pl.ds(kh, PAGE, stride=4), :]


def _attn_a_s_kernel(pt_ref, lam_ref, q_ref, *rest, pps, n_steps):
    kwin = [lambda kh, w=w: _head_rows(w, kh) for w in rest[:pps]]
    vwin = [lambda kh, w=w: _head_rows(w, kh) for w in rest[pps:2 * pps]]
    new_ref, blast_ref, bnew_ref, cfar_ref, o_ref, m_ref, l_ref, acc_ref = rest[2 * pps:]
    j = pl.program_id(1)
    last = j == n_steps - 1

    @pl.when(j == 0)
    def _():
        _fa_init(m_ref, l_ref, acc_ref)

    s, vgroups = [], []
    for kh in range(A_KVH):
        q = q_ref[kh]
        far = cfar_ref[kh]
        row = [_dot_t(q, w(kh).astype(BF16)) + far for w in kwin[:-1]]
        row.append(_dot_t(q, kwin[-1](kh).astype(BF16)) + jnp.where(last, blast_ref[kh], far))
        s.append(jnp.concatenate(row, axis=1))
        vgroups.append([w(kh).astype(BF16) for w in vwin])
    _fa_update(jnp.concatenate(s, axis=0), vgroups, m_ref, l_ref, acc_ref)

    @pl.when(last)
    def _():
        lam = lam_ref[0]
        s, vgroups = [], []
        for kh in range(A_KVH):
            kn = _pad_rows(new_ref[:, kh * 256:kh * 256 + 128], LANE).astype(BF16)
            s.append(_dot_t(q_ref[kh], kn) + bnew_ref[kh])
            vgroups.append([_pad_rows(new_ref[:, kh * 256 + 128:(kh + 1) * 256], LANE).astype(BF16)])
        _fa_update(jnp.concatenate(s, axis=0), vgroups, m_ref, l_ref, acc_ref)
        o = acc_ref[...] / l_ref[...]
        for kh in range(A_KVH):
            for g in range(2):
                r = kh * 32 + 16 * g
                o_ref[kh, g * 8:(g + 1) * 8, :] = o[r:r + 8] - lam * o[r + 8:r + 16]


def _kv_page_specs(layer, pps):
    def spec(u, half):
        return pl.BlockSpec((None, None, PAGE, 4, LANE),
                            lambda s, j, pt: (layer, pt[s, j * pps + u], 0, 0, half))
    return [spec(u, 0) for u in range(pps)] + [spec(u, 1) for u in range(pps)]


def _page_specs(layer, pps):
    return [pl.BlockSpec((None, None, PAGE * SUBROWS, LANE),
                         lambda s, j, pt, u=u: (layer, pt[s, j * pps + u], 0, 0)) for u in range(pps)]


def _attn_a_s(pt, lam, qs, cache, layer, new, blast, bnew, cfar, pps):
    n_seq, n_pages = pt.shape
    n_steps = n_pages // pps
    c3 = lambda a: pl.BlockSpec(a.shape, lambda s, j, pt: (0, 0, 0))
    grid_spec = pltpu.PrefetchScalarGridSpec(
        num_scalar_prefetch=1,
        grid=(n_seq, n_steps),
        in_specs=[_smem(), pl.BlockSpec((None, 4, 32, 128), lambda s, j, pt: (s, 0, 0, 0))]
        + _kv_page_specs(layer, pps)
        + [pl.BlockSpec((None, 8, 1024), lambda s, j, pt: (s, 0, 0)), c3(blast), c3(bnew), c3(cfar)],
        out_specs=pl.BlockSpec((None, 4, 16, 128), lambda s, j, pt: (s, 0, 0, 0)),
        scratch_shapes=_fa_scratch(A_KVH * 32),
    )
    return pl.pallas_call(
        functools.partial(_attn_a_s_kernel, pps=pps, n_steps=n_steps),
        grid_spec=grid_spec,
        out_shape=jax.ShapeDtypeStruct((n_seq, 4, 16, 128), F32),
        compiler_params=_cp(("parallel", "arbitrary")),
        name="attn_a_s",
    )(pt, lam, qs, *([cache] * (2 * pps)), new, blast, bnew, cfar)


def _attn_b_s_kernel(pt_ref, q_ref, cq_ref, *rest, pps, n_steps):
    kwin, vwin = rest[:pps], rest[pps:2 * pps]
    ck_ref, ckn_ref, new_ref, tri_ref, o_ref, m_ref, l_ref, acc_ref = rest[2 * pps:]
    j = pl.program_id(1)

    @pl.when(j == 0)
    def _():
        _fa_init(m_ref, l_ref, acc_ref)

    def decay(ck, kh, lo):
        rows = [jnp.broadcast_to(ck[kh, g:g + 1, lo:lo + LANE], (8, LANE)) for g in range(2)]
        return cq_ref[kh] - jnp.concatenate(rows, axis=0)

    s, vgroups = [], []
    for kh in range(B_KVH):
        s.append(jnp.concatenate([_dot_t(q_ref[kh], _head_rows(w, kh).astype(BF16)) + decay(ck_ref, kh, u * LANE)
                                  for u, w in enumerate(kwin)], axis=1))
        vgroups.append([_head_rows(w, kh).astype(BF16) for w in vwin])
    _fa_update(jnp.concatenate(s, axis=0), vgroups, m_ref, l_ref, acc_ref)

    @pl.when(j == n_steps - 1)
    def _():
        s, vgroups = [], []
        for kh in range(B_KVH):
            kn = _pad_rows(new_ref[:, kh * 256:kh * 256 + 128], LANE).astype(BF16)
            s.append(_dot_t(q_ref[kh], kn) + decay(ckn_ref, kh, 0) + tri_ref[...])
            vgroups.append([_pad_rows(new_ref[:, kh * 256 + 128:(kh + 1) * 256], LANE).astype(BF16)])
        _fa_update(jnp.concatenate(s, axis=0), vgroups, m_ref, l_ref, acc_ref)
        o = acc_ref[...] / l_ref[...]
        for kh in range(B_KVH):
            o_ref[kh] = o[kh * 16:(kh + 1) * 16]


def _attn_b_s(pt, qs, cqb, cache, layer, cst, new, tri, pps):
    n_seq, n_pages = pt.shape
    n_steps = n_pages // pps
    grid_spec = pltpu.PrefetchScalarGridSpec(
        num_scalar_prefetch=1,
        grid=(n_seq, n_steps),
        in_specs=[pl.BlockSpec((None, 4, 16, 128), lambda s, j, pt: (s, 0, 0, 0)),
                  pl.BlockSpec((None, 4, 16, 128), lambda s, j, pt: (s, 0, 0, 0))]
        + _kv_page_specs(layer, pps)
        + [pl.BlockSpec((None, 4, 2, pps * 128), lambda s, j, pt: (s, 0, 0, j)),
           pl.BlockSpec((None, 4, 2, 128), lambda s, j, pt: (s, 0, 0, n_pages)),
           pl.BlockSpec((None, 8, 1024), lambda s, j, pt: (s, 0, 0)),
           pl.BlockSpec(tri.shape, lambda s, j, pt: (0, 0))],
        out_specs=pl.BlockSpec((None, 4, 16, 128), lambda s, j, pt: (s, 0, 0, 0)),
        scratch_shapes=_fa_scratch(B_KVH * 16),
    )
    return pl.pallas_call(
        functools.partial(_attn_b_s_kernel, pps=pps, n_steps=n_steps),
        grid_spec=grid_spec,
        out_shape=jax.ShapeDtypeStruct((n_seq, 4, 16, 128), F32),
        compiler_params=_cp(("parallel", "arbitrary")),
        name="attn_b_s",
    )(pt, qs, cqb, *([cache] * (2 * pps)), cst, cst, new, tri)


def _nsa_s1_kernel(pt_ref, q_ref, *rest, pps, n_steps, n_cmp, n_sel, nsp, q0):
    pages = rest[:pps]
    pe_ref, w1_ref, w2_ref, kn_ref, bc_ref, ocmp_ref, sel_ref, xk_ref, xv_ref, flat_ref = rest[pps:]
    j = pl.program_id(1)
    for u, w in enumerate(pages):
        r0 = pl.multiple_of((j * pps + u) * PAGE, PAGE)
        for kh in range(C_KVH):
            xk_ref[kh, pl.ds(r0, PAGE), :] = _sub(w, 4 * kh)
            xv_ref[kh, pl.ds(r0, PAGE), :] = _sub(w, 4 * kh + 1)

    @pl.when(j == n_steps - 1)
    def _():
        for kh in range(C_KVH):
            kc, vc = _compress((xk_ref.at[kh], xv_ref.at[kh]), flat_ref, pe_ref, w1_ref, w2_ref, kn_ref, n_cmp)
            q = q_ref[kh]
            qg = [q[g * 8:(g + 1) * 8] for g in range(C_GRP)]
            ocmp, sel = _cmp_and_select(qg, kc.astype(BF16), vc.astype(BF16),
                                        lambda g, kh=kh: bc_ref[kh, g * 8:(g + 1) * 8], q0, 8, nsp, n_sel)
            for g in range(C_GRP):
                ocmp_ref[kh, g * 8:(g + 1) * 8, :] = ocmp[g]
            sel_ref[kh] = sel


def _nsa_s1(pt, qs, cache, layer, pe, w1, w2, kn0, bias_cmp, pps, n_sel, nsp, q0):
    n_seq, n_pages = pt.shape
    n_steps = n_pages // pps
    n_cmp = n_pages * PAGE // L_BLK
    whole = lambda a: pl.BlockSpec(a.shape, lambda s, j, pt: (0,) * a.ndim)
    per_s = lambda r, c: pl.BlockSpec((None, C_KVH, r, c), lambda s, j, pt: (s, 0, 0, 0))
    grid_spec = pltpu.PrefetchScalarGridSpec(
        num_scalar_prefetch=1,
        grid=(n_seq, n_steps),
        in_specs=[per_s(32, 128)] + _page_specs(layer, pps)
        + [whole(pe), whole(w1), whole(w2), whole(kn0), whole(bias_cmp)],
        out_specs=[per_s(32, 128), per_s(8, nsp)],
        scratch_shapes=[pltpu.VMEM((C_KVH, n_pages * PAGE, 128), F32), pltpu.VMEM((C_KVH, n_pages * PAGE, 128), F32),
                        pltpu.VMEM((n_cmp, L_BLK * C_HD), BF16)],
    )
    return pl.pallas_call(
        functools.partial(_nsa_s1_kernel, pps=pps, n_steps=n_steps, n_cmp=n_cmp, n_sel=n_sel, nsp=nsp, q0=q0),
        grid_spec=grid_spec,
        out_shape=[jax.ShapeDtypeStruct((n_seq, C_KVH, 32, 128), F32),
                   jax.ShapeDtypeStruct((n_seq, C_KVH, 8, nsp), F32)],
        compiler_params=_cp(("parallel", "arbitrary")),
        name="nsa_s1",
    )(pt, qs, *([cache] * pps), pe, w1, w2, kn0, bias_cmp)


def _nsa_s2_kernel(pt_ref, q_ref, *rest, pps, n_steps):
    pages = rest[:pps]
    (am_ref, amn_ref, new_ref, wst_ref, wnew_ref, blast_ref, bnew_ref, cfar_ref, bwin_ref, gt_ref, ocmp_ref,
     o_ref, m_ref, l_ref, acc_ref) = rest[pps:]
    j = pl.program_id(1)
    last = j == n_steps - 1

    @pl.when(j == 0)
    def _():
        _fa_init(m_ref, l_ref, acc_ref)

    s, vgroups = [], []
    for kh in range(C_KVH):
        q = q_ref[kh]
        far = cfar_ref[kh]
        row = []
        for u, w in enumerate(pages):
            bias = jnp.where(last, blast_ref[kh], far) if u == pps - 1 else far
            row.append(_dot_t(q, _sub(w, 4 * kh + 2).astype(BF16)) + bias
                       + _rows(am_ref[kh, :, u * LANE:(u + 1) * LANE], C_GRP))
        s.append(jnp.concatenate(row, axis=1))
        vgroups.append([_sub(w, 4 * kh + 3).astype(BF16) for w in pages])
    _fa_update(jnp.concatenate(s, axis=0), vgroups, m_ref, l_ref, acc_ref)

    @pl.when(last)
    def _():
        s, vgroups = [], []
        for kh in range(C_KVH):
            c0 = kh * 512 + 256
            kn = _pad_rows(new_ref[:, c0:c0 + 128], LANE).astype(BF16)
            s.append(_dot_t(q_ref[kh], kn) + bnew_ref[kh] + _rows(amn_ref[kh], C_GRP))
            vgroups.append([_pad_rows(new_ref[:, c0 + 128:c0 + 256], LANE).astype(BF16)])
        _fa_update(jnp.concatenate(s, axis=0), vgroups, m_ref, l_ref, acc_ref)
        o_sel_all = acc_ref[...] / l_ref[...]
        for kh in range(C_KVH):
            q = q_ref[kh]
            o_sel = o_sel_all[kh * 32:(kh + 1) * 32]
            w0 = kh * 256
            wk = wst_ref[:, w0:w0 + 128].astype(BF16)
            wv = wst_ref[:, w0 + 128:w0 + 256].astype(BF16)
            wkn = _pad_rows(wnew_ref[:, w0:w0 + 128], LANE).astype(BF16)
            wvn = _pad_rows(wnew_ref[:, w0 + 128:w0 + 256], LANE).astype(BF16)
            nst = wk.shape[0]
            sw = jnp.concatenate([_dot_t(q, wk) + bwin_ref[kh], _dot_t(q, wkn) + bnew_ref[kh]], axis=1)
            p = jnp.exp(sw - jnp.max(sw, axis=-1, keepdims=True))
            o_win = (jnp.dot(p[:, 0:nst].astype(BF16), wv, preferred_element_type=F32)
                     + jnp.dot(p[:, nst:].astype(BF16), wvn, preferred_element_type=F32)) / jnp.sum(p, axis=-1, keepdims=True)
            gt = gt_ref[kh]
            o_ref[kh] = gt[:, 0:128] * ocmp_ref[kh] + gt[:, 128:256] * o_sel + gt[:, 256:384] * o_win


def _nsa_s2(pt, qs, cache, layer, addmask, c_new8, win_state, w_new8, blast, bnew, cfar, bwin, gates, ocmp, pps):
    n_seq, n_pages = pt.shape
    n_steps = n_pages // pps
    nst = win_state.shape[2]
    whole = lambda a: pl.BlockSpec(a.shape, lambda s, j, pt: (0,) * a.ndim)
    per_s = lambda r, c: pl.BlockSpec((None, C_KVH, r, c), lambda s, j, pt: (s, 0, 0, 0))
    grid_spec = pltpu.PrefetchScalarGridSpec(
        num_scalar_prefetch=1,
        grid=(n_seq, n_steps),
        in_specs=[per_s(32, 128)] + _page_specs(layer, pps)
        + [pl.BlockSpec((None, C_KVH, 8, pps * 128), lambda s, j, pt: (s, 0, 0, j)),
           pl.BlockSpec((None, C_KVH, 8, 128), lambda s, j, pt: (s, 0, 0, n_pages)),
           pl.BlockSpec((None, 8, c_new8.shape[2]), lambda s, j, pt: (s, 0, 0)),
           pl.BlockSpec((None, None, nst, win_state.shape[3]), lambda s, j, pt: (layer, s, 0, 0)),
           pl.BlockSpec((None, 8, w_new8.shape[2]), lambda s, j, pt: (s, 0, 0)),
           whole(blast), whole(bnew), whole(cfar), whole(bwin), per_s(32, 384), per_s(32, 128)],
        out_specs=per_s(32, 128),
        scratch_shapes=_fa_scratch(C_KVH * 32),
    )
    return pl.pallas_call(
        functools.partial(_nsa_s2_kernel, pps=pps, n_steps=n_steps),
        grid_spec=grid_spec,
        out_shape=jax.ShapeDtypeStruct((n_seq, C_KVH, 32, 128), F32),
        compiler_params=_cp(("parallel", "arbitrary")),
        name="nsa_s2",
    )(pt, qs, *([cache] * pps), addmask, addmask, c_new8, win_state, w_new8, blast, bnew, cfar, bwin, gates, ocmp)


def _final_kernel(x_ref, oa_ref, ob_ref, oc_ref, ga_ref, gb_ref, gc_ref, mg_ref, sub_ref, wb_ref, wo_ref, y_ref,
                  *, c_sub, d):
    y = None
    for n, (o_ref, g_ref) in enumerate(((oa_ref, ga_ref), (ob_ref, gb_ref), (oc_ref, gc_ref))):
        parts = []
        for h in range(8):
            sl = slice(h * LANE, (h + 1) * LANE)
            o = o_ref[:, sl].astype(F32)
            if n == 0:
                o = _rms(o, sub_ref[...]) * c_sub
            g = g_ref[:, sl].astype(F32)
            parts.append((o * (g * _sigmoid(g))).astype(BF16))
        proj = jnp.dot(jnp.concatenate(parts, axis=1), wb_ref[n], preferred_element_type=F32)
        term = _sigmoid(mg_ref[:, n * d:(n + 1) * d].astype(F32)) * proj
        y = term if y is None else y + term
    y_ref[...] = x_ref[...] + jnp.dot(y.astype(BF16), wo_ref[...], preferred_element_type=F32)


def _final(x, oa, ob, oc, z, sub, wb, wo, c_sub):
    m, d = x.shape
    tm = min(m, 256)
    row = lambda w, blk=0: pl.BlockSpec((tm, w), lambda i, blk=blk: (i, blk))
    once = lambda a: pl.BlockSpec(a.shape, lambda i: (0,) * a.ndim, pipeline_mode=pl.Buffered(1))
    return pl.pallas_call(
        functools.partial(_final_kernel, c_sub=c_sub, d=d),
        grid=(m // tm,),
        in_specs=[row(d), row(1024), row(1024), row(1024), row(1024, BLK_AG), row(1024, BLK_BG), row(1024, BLK_CG),
                  row(3 * d, 0), once(sub), once(wb), once(wo)],
        out_specs=row(d),
        out_shape=jax.ShapeDtypeStruct((m, d), F32),
        compiler_params=_cp(("parallel",), 56),
        name="final",
    )(x, oa, ob, oc, z, z, z, z, sub, wb, wo)


def _bucket_np(dist):
    n = np.maximum(dist, 0)
    exact = REL_BUCKETS // 2
    nf = np.maximum(n, exact).astype(np.float32)
    big = exact + (np.log(nf / np.float32(exact)) / np.float32(math.log(REL_MAX_DIST / exact))
                   * np.float32(REL_BUCKETS - exact)).astype(np.int32)
    return np.where(n < exact, n, np.minimum(big, REL_BUCKETS - 1)).astype(np.int32)


def _rel_table(rel, dist, valid):
    tab = jnp.take(rel.astype(F32), jnp.asarray(_bucket_np(dist)), axis=0)
    tab = jnp.where(jnp.asarray(valid)[..., None], tab, NEG)
    return jnp.moveaxis(tab, -1, 0)


def _toeplitz(rel, off, n, valid):
    d = off + np.arange(n - 1, -n, -1)
    w = _rel_table(rel, d, valid(d))
    w = jnp.pad(w, ((0, 0), (0, 1)))
    skew = jnp.tile(w, (1, n))[:, :n * (2 * n - 1)].reshape(-1, n, 2 * n - 1)
    return skew[:, :, n - 1:]


def _prompt_tables(rel, tq, with_window):
    tabs = [_toeplitz(rel, 0, tq, lambda d: d >= 0), _toeplitz(rel, tq, tq, lambda d: d >= 0)]
    if with_window:
        tabs.append(_toeplitz(rel, WINDOW, tq, lambda d: d <= WINDOW))
    return jnp.stack(tabs, axis=1)


def _cmp_table(rel, t, n_cmp, ncp):
    assert t == n_cmp * L_BLK
    n = n_cmp
    d = L_BLK * np.arange(n - 1, -n, -1)[None, :] + np.arange(L_BLK)[:, None] - (L_BLK - 1)
    w = jnp.pad(_rel_table(rel, d, d >= 0), ((0, 0), (0, 0), (0, 1)))
    skew = jnp.tile(w, (1, 1, n))[:, :, :n * (2 * n - 1)].reshape(-1, L_BLK, n, 2 * n - 1)[..., n - 1:]
    tab = jnp.transpose(skew, (0, 2, 1, 3)).reshape(-1, t, n)
    return jnp.pad(tab, ((0, 0), (0, 0), (0, ncp - n_cmp)), constant_values=NEG)


def _layer_weights(l, rms_gain, w_in, b_forget, a_q_norm, a_k_norm, a_lambda, a_subln, b_q_norm, b_k_norm, c_q_norm,
                   c_k_norm, c_cmp_pe, c_cmp_w1, c_cmp_w2, w_branch, w_out):
    w = w_in[l]
    d = w.shape[0]
    w_perm = jnp.concatenate([w[:, a:b] for a, b in _PERM] + [jnp.zeros((d, N_PAD - N_USED), F32)],
                             axis=1).astype(BF16)
    two = lambda g: jnp.concatenate([g, g])
    pn = jnp.stack([two(a_q_norm[l]), two(a_k_norm[l]), b_q_norm[l], b_k_norm[l], c_q_norm[l],
                    c_k_norm[l, 0], c_k_norm[l, 1], c_k_norm[l, 2]]).astype(F32)
    bf = jnp.zeros((1, LANE), F32).at[0, :B_HEADS].set(b_forget[l])
    al = a_lambda[l].astype(F32)
    lam_init = 0.8 - 0.6 * math.exp(-0.3 * l)
    lam = (jnp.exp(jnp.sum(al[0] * al[1])) - jnp.exp(jnp.sum(al[2] * al[3])) + lam_init).reshape(1)
    return dict(g=rms_gain[l], w=w_perm, pn=pn, bf=bf, lam=lam, c_sub=1.0 - lam_init,
                sub=a_subln[l].reshape(1, LANE), pe=c_cmp_pe[l], w1=c_cmp_w1[l].astype(BF16),
                w2=c_cmp_w2[l].astype(BF16), kn0=c_k_norm[l, 0].reshape(1, LANE),
                wb=w_branch[l].astype(BF16), wo=w_out[l].astype(BF16))


def _prompt_layer(x, lw, tb, tq):
    bsz, t, d = x.shape
    m = bsz * t
    zg, zr = _proj(x.reshape(m, d), lw["g"], lw["w"])
    qa, qb, qc, a_new, b_new, c_new, w_new, sm = _post(zr, lw["pn"], lw["bf"])
    r3 = lambda a: a.reshape(bsz, t, a.shape[-1])
    qa, qb, qc, a_new, b_new, c_new, w_new, sm = map(r3, (qa, qb, qc, a_new, b_new, c_new, w_new, sm))
    logf = sm[..., :B_HEADS]
    o_a = _attn_a(qa, a_new, tb["a_tabs"], tb["a_far"], lw["lam"], tq)
    csum = jnp.cumsum(logf, axis=1).reshape(bsz, t, B_KVH, 2)
    o_b = _attn_b(qb, b_new, jnp.transpose(csum, (0, 2, 1, 3)), jnp.transpose(csum, (0, 2, 3, 1)), tb["tri"], tq)
    kc, vc = _cmp_p(c_new, lw["pe"], lw["w1"], lw["w2"], lw["kn0"], tb["ncp"])
    gates = jnp.transpose(sm[..., 8:8 + 3 * C_HEADS].reshape(bsz, t, C_KVH, 3 * C_GRP), (0, 2, 1, 3))
    o_c = _nsa_p(qc, kc, vc, c_new, w_new, gates, tb["c_cmp"], tb["c_tabs"], tb["expand"], tb["c_far"], tq)
    y = _final(x.reshape(m, d), o_a.reshape(m, 1024), o_b.reshape(m, 1024), o_c.reshape(m, 1024), zg,
               lw["sub"], lw["wb"], lw["wo"], lw["c_sub"])
    keep = min(WINDOW, t)
    return (y.reshape(bsz, t, d), a_new.reshape(bsz, t, A_KVH, 256), b_new.reshape(bsz, t, B_KVH, 256), logf,
            c_new.reshape(bsz, t, C_KVH, 4, C_HD), w_new[:, t - keep:].reshape(bsz, keep, C_KVH, 2, C_HD))


def _stack_q(q, n_seq, t, kvh, grp):
    q = q.reshape(n_seq, t, kvh, grp, LANE)
    q = jnp.pad(q, ((0, 0), (0, 8 - t), (0, 0), (0, 0), (0, 0)))
    return jnp.transpose(q, (0, 2, 3, 1, 4)).reshape(n_seq, kvh, grp * 8, LANE)


def _unstack_o(o, n_seq, t, kvh, grp):
    o = o.reshape(n_seq, kvh, grp, 8, LANE)[:, :, :, :t]
    return jnp.transpose(o, (0, 3, 1, 2, 4)).reshape(n_seq * t, kvh * grp * LANE)


def _sample_layer(x, l, lw, tb, caches, page_table, pps):
    n_seq, t, d = x.shape
    m = n_seq * t
    cache_a, cache_b, cache_lf, cache_c, win_state = caches
    n_pages = page_table.shape[1]
    past = n_pages * PAGE
    zg, zr = _proj(x.reshape(m, d), lw["g"], lw["w"])
    qa, qb, qc, a_new, b_new, c_new, w_new, sm = _post(zr, lw["pn"], lw["bf"])
    logf = sm[:, :B_HEADS].reshape(n_seq, t, B_HEADS)
    pad8 = lambda a: jnp.pad(a.reshape(n_seq, t, a.shape[-1]), ((0, 0), (0, 8 - t), (0, 0)))
    a_new8, b_new8, c_new8, w_new8 = map(pad8, (a_new, b_new, c_new, w_new))
    lo = np.arange(LANE) < A_HD
    qa_m = jnp.where(jnp.asarray(np.stack([lo, ~lo])), qa.reshape(n_seq, t, A_KVH, 2, 1, LANE), 0)
    qs_a = _stack_q(qa_m.reshape(m, A_KVH * 4 * LANE), n_seq, t, A_KVH, 4)
    o_a = _attn_a_s(page_table, lw["lam"], qs_a, cache_a, l, a_new8, tb["a_last"], tb["a_new"], tb["a_far"], pps)
    o_a = _unstack_o(o_a, n_seq, t, A_KVH, 2)
    lf_past = cache_lf[l][page_table].reshape(n_seq, past, B_HEADS).astype(F32)
    csum = jnp.cumsum(jnp.concatenate([lf_past, logf], axis=1), axis=1)
    cst = jnp.pad(jnp.transpose(csum, (0, 2, 1)), ((0, 0), (0, 0), (0, 128 - t))).reshape(n_seq, B_KVH, 2, past + 128)
    cq = jnp.pad(csum[:, past:], ((0, 0), (0, 8 - t), (0, 0))).reshape(n_seq, 8, B_KVH, 2)
    cqb = jnp.broadcast_to(jnp.transpose(cq, (0, 2, 3, 1)).reshape(n_seq, B_KVH, 16, 1), (n_seq, B_KVH, 16, LANE))
    o_b = _attn_b_s(page_table, _stack_q(qb, n_seq, t, B_KVH, 2), cqb, cache_b, l, cst, b_new8, tb["tri"], pps)
    o_b = _unstack_o(o_b, n_seq, t, B_KVH, 2)
    qs_c = _stack_q(qc, n_seq, t, C_KVH, C_GRP)
    n_sel = -(-(past + t) // L_BLK)
    ocmp, sel = _nsa_s1(page_table, qs_c, cache_c, l, lw["pe"], lw["w1"], lw["w2"], lw["kn0"], tb["c_cmp"], PPS_CMP,
                        n_sel, tb["nsp"], past)
    selx = jnp.repeat(sel[..., :n_sel], L_BLK, axis=-1)
    addmask = (jnp.pad(selx, ((0, 0),) * 3 + ((0, past + 128 - n_sel * L_BLK),)) - 1.0) * (-NEG)
    g = jnp.pad(sm[:, 8:8 + 3 * C_HEADS].reshape(n_seq, t, C_KVH, C_GRP, 3), ((0, 0), (0, 8 - t)) + ((0, 0),) * 3)
    g = jnp.transpose(g, (0, 2, 4, 3, 1)).reshape(n_seq, C_KVH, 3, 32, 1)
    gates = jnp.transpose(jnp.broadcast_to(g, (n_seq, C_KVH, 3, 32, LANE)), (0, 1, 3, 2, 4)).reshape(n_seq, C_KVH, 32, 384)
    o_c = _nsa_s2(page_table, qs_c, cache_c, l, addmask, c_new8, win_state, w_new8, tb["c_last"], tb["c_new"],
                  tb["c_far"], tb["c_win"], gates, ocmp, pps)
    o_c = _unstack_o(o_c, n_seq, t, C_KVH, C_GRP)
    y = _final(x.reshape(m, d), o_a, o_b, o_c, zg, lw["sub"], lw["wb"], lw["wo"], lw["c_sub"])
    w_all = jnp.concatenate([win_state[l].reshape(n_seq, -1, C_KVH, 2, C_HD), w_new.reshape(n_seq, t, C_KVH, 2, C_HD)], 1)
    keep = win_state.shape[2]
    return (y.reshape(n_seq, t, d), a_new.reshape(n_seq, t, A_KVH, 256), b_new.reshape(n_seq, t, B_KVH, 256), logf,
            c_new.reshape(n_seq, t, C_KVH, 4, C_HD), w_all[:, w_all.shape[1] - keep:])


def _sample_tables(rel_a, rel_c, t, past, n_cmp, nst):
    tok = np.arange(8)
    jj = np.arange(LANE)
    d_last = LANE + tok[:, None] - jj[None, :]
    d_new = tok[:, None] - jj[None, :]
    v_new = (d_new >= 0) & (jj[None, :] < t)
    ones = np.ones((8, LANE), bool)

    def rows(rel, dist, valid, kvh, per):
        tab = _rel_table(rel, dist, valid)
        return tab.reshape(kvh, per * 8, dist.shape[1])

    tb = dict(a_last=rows(rel_a, d_last, ones, A_KVH, 4), a_new=rows(rel_a, d_new, v_new, A_KVH, 4),
              c_last=rows(rel_c, d_last, ones, C_KVH, C_GRP), c_new=rows(rel_c, d_new, v_new, C_KVH, C_GRP))
    far = lambda rel, kvh, per: jnp.broadcast_to(
        jnp.repeat(rel[REL_BUCKETS - 1].astype(F32), 8).reshape(kvh, per * 8, 1), (kvh, per * 8, LANE))
    tb["a_far"] = far(rel_a, A_KVH, 4)
    tb["c_far"] = far(rel_c, C_KVH, C_GRP)
    ws = np.arange(nst)
    d_win = nst + tok[:, None] - ws[None, :]
    tb["c_win"] = rows(rel_c, d_win, (d_win >= 0) & (d_win <= WINDOW), C_KVH, C_GRP)
    d_cmp = past + tok[:, None] - (L_BLK * np.arange(n_cmp)[None, :] + L_BLK - 1)
    tb["c_cmp"] = rows(rel_c, d_cmp, d_cmp >= 0, C_KVH, C_GRP)
    tri = np.where(v_new, 0.0, NEG).astype(np.float32)
    tb["tri"] = jnp.asarray(np.concatenate([tri, tri], axis=0))
    return tb


def kernel(x_prompt, x_sample, cache_a_kv, cache_b_kv, cache_b_logf, cache_c_kv, state_c_win, page_table, rms_gain,
           w_in, b_forget, a_q_norm, a_k_norm, a_lambda, a_subln, b_q_norm, b_k_norm, c_q_norm, c_k_norm, c_cmp_pe,
           c_cmp_w1, c_cmp_w2, rel_bias, w_branch, w_out):
    depth = w_in.shape[0]
    bsz, t, d = x_prompt.shape
    n_seq, t_s, _ = x_sample.shape
    n_pages = page_table.shape[1]
    past = n_pages * cache_a_kv.shape[2]
    nst = state_c_win.shape[2]
    assert cache_a_kv.shape[2] == PAGE and past % L_BLK == 0 and t_s <= 8 and nst == WINDOW and past >= WINDOW
    tq = min(256, t)
    assert t % tq == 0 and WINDOW % tq == 0 and WINDOW // tq >= 2 and tq % L_BLK == 0
    pps = PPS_ATTN
    assert n_pages % PPS_ATTN == 0 and n_pages % PPS_CMP == 0

    rel_a = rel_bias[:, :2 * A_HEADS]
    rel_c = rel_bias[:, 2 * A_HEADS:]
    n_cmp_p = t // L_BLK
    ncp = -(-n_cmp_p // LANE) * LANE
    nsp_p = -(-(-(-t // L_BLK)) // LANE) * LANE
    tri = np.where(np.arange(tq)[:, None] >= np.arange(tq)[None, :], 0.0, NEG).astype(np.float32)
    expand = (np.arange(t)[None, :] // L_BLK == np.arange(nsp_p)[:, None]).astype(np.float32)
    tb_p = dict(
        a_tabs=_prompt_tables(rel_a, tq, False).reshape(A_KVH, 4, 2, tq, tq).transpose(0, 2, 1, 3, 4)
        .reshape(A_KVH, 2, 4 * tq, tq),
        a_far=rel_a[REL_BUCKETS - 1].astype(F32),
        c_tabs=_prompt_tables(rel_c, tq, True).reshape(C_KVH, C_GRP, 3, tq, tq).transpose(0, 2, 1, 3, 4)
        .reshape(C_KVH, 3, C_GRP * tq, tq),
        c_far=rel_c[REL_BUCKETS - 1].astype(F32),
        c_cmp=_cmp_table(rel_c, t, n_cmp_p, ncp),
        tri=jnp.asarray(np.concatenate([tri, tri], axis=0)), expand=jnp.asarray(expand, dtype=BF16), ncp=ncp)
    n_sel_s = -(-(past + t_s) // L_BLK)
    tb_s = _sample_tables(rel_a, rel_c, t_s, past, past // L_BLK, nst)
    tb_s["nsp"] = -(-n_sel_s // LANE) * LANE

    caches = (cache_a_kv, cache_b_kv, cache_b_logf,
              cache_c_kv.reshape(depth, cache_c_kv.shape[1], PAGE * SUBROWS, LANE),
              state_c_win.reshape(depth, n_seq, nst, C_KVH * 2 * C_HD))
    yp, ys = x_prompt, x_sample
    st_p, st_s = [], []
    for l in range(depth):
        lw = _layer_weights(l, rms_gain, w_in, b_forget, a_q_norm, a_k_norm, a_lambda, a_subln, b_q_norm, b_k_norm,
                            c_q_norm, c_k_norm, c_cmp_pe, c_cmp_w1, c_cmp_w2, w_branch, w_out)
        yp, *sp = _prompt_layer(yp, lw, tb_p, tq)
        ys, *ss = _sample_layer(ys, l, lw, tb_s, caches, page_table, pps)
        st_p.append(sp)
        st_s.append(ss)
    stk = lambda lst, i: jnp.stack([s[i] for s in lst])
    return (yp, ys, stk(st_p, 0), stk(st_s, 0), stk(st_p, 1), stk(st_s, 1), stk(st_p, 2), stk(st_s, 2),
            stk(st_p, 3), stk(st_s, 3), stk(st_p, 4), stk(st_s, 4))
```
